```python
import jax
import jax.numpy as jnp
from jax import lax
import numpy as np

D_MODEL = 1024
BATCH = 8
SEQ = 8192
DEPTH = 1
DEC_BATCH = 16
DEC_SEQ = 2048
PAST_LEN = 128

GLA_HEADS = 4
GLA_DK = 64
GLA_DV = 128
GLA_GATE_RANK = 16
GLA_GATE_NORMALIZER = 16.0
GLA_CHUNK = 64
FNET_GROUPS = 4
FNET_GROUP_DIM = 128
PEER_HEADS = 8
PEER_NKEYS = 128
PEER_EXPERTS = PEER_NKEYS * PEER_NKEYS
PEER_QDIM = 256
PEER_TOPK = 16
PEER_TOKEN_BLOCK = 128
PLE_DIM = 256
LN_EPS = 1e-5
RMS_EPS = 1e-6
DEEPNORM_ALPHA = (2.0 * DEPTH) ** 0.25
DEEPNORM_BETA = (8.0 * DEPTH) ** -0.25

QK_WIDTH = GLA_HEADS * GLA_DK
V_WIDTH = GLA_HEADS * GLA_DV
F_WIDTH = FNET_GROUPS * FNET_GROUP_DIM
OFF_Q = 0
OFF_K = OFF_Q + QK_WIDTH
OFF_V = OFF_K + QK_WIDTH
OFF_R = OFF_V + V_WIDTH
OFF_AF = OFF_R + V_WIDTH
OFF_AB = OFF_AF + GLA_GATE_RANK
OFF_F = OFF_AB + GLA_GATE_RANK
IN_COLS = OFF_F + F_WIDTH

kernel_name = 'hybrid_gla_fnet_peer_encoder'


def layer_norm(x, g, b):
    xf = x.astype(jnp.float32)
    mu = jnp.mean(xf, axis=-1, keepdims=True)
    var = jnp.mean(jnp.square(xf - mu), axis=-1, keepdims=True)
    y = (xf - mu) * lax.rsqrt(var + LN_EPS) * g.astype(jnp.float32) + b.astype(jnp.float32)
    return y.astype(x.dtype)


def gla_one_direction(q, k, v, log_a, strict):
    bsz, nh, seq, dk = q.shape
    dv = v.shape[-1]
    c = GLA_CHUNK
    n = seq // c
    f32 = jnp.float32
    qc = q.astype(f32).reshape(bsz, nh, n, c, dk)
    kc = k.astype(f32).reshape(bsz, nh, n, c, dk)
    vc = v.astype(f32).reshape(bsz, nh, n, c, dv)
    bc = jnp.cumsum(log_a.astype(f32).reshape(bsz, nh, n, c, dk), axis=3)
    b_last = bc[:, :, :, c - 1:c, :]
    b_mid = bc[:, :, :, c // 2 - 1:c // 2, :]
    scores = jnp.einsum('bhncd,bhnsd->bhncs', qc * jnp.exp(bc - b_mid), kc * jnp.exp(b_mid - bc))
    mask = jnp.tril(jnp.ones((c, c), dtype=bool), k=-1 if strict else 0)
    scores = jnp.where(mask, scores, 0.0)
    o_intra = jnp.einsum('bhncs,bhnsv->bhncv', scores, vc)
    q_in = qc * jnp.exp(bc)
    k_st = kc * jnp.exp(b_last - bc)
    chunk_decay = jnp.exp(b_last[:, :, :, 0, :])

    def step(state, xs):
        q_n, k_n, v_n, d_n = xs
        o_n = jnp.einsum('bhcd,bhdv->bhcv', q_n, state)
        state = d_n[..., None] * state + jnp.einsum('bhcd,bhcv->bhdv', k_n, v_n)
        return state, o_n

    xs = (jnp.moveaxis(q_in, 2, 0), jnp.moveaxis(k_st, 2, 0),
          jnp.moveaxis(vc, 2, 0), jnp.moveaxis(chunk_decay, 2, 0))
    s0 = jnp.zeros((bsz, nh, dk, dv), f32)
    _, o_inter = lax.scan(step, s0, xs)
    o = o_intra + jnp.moveaxis(o_inter, 0, 2)
    return o.reshape(bsz, nh, seq, dv)


def token_mixer(x, w_in, w_af2, b_af, w_ab2, b_ab, gla_norm_g, w_ao, w_fo, w_mg, b_mg, w_o):
    bsz, seq, _ = x.shape
    f32 = jnp.float32
    h = x @ w_in

    def to_heads(t, d):
        return t.reshape(bsz, seq, GLA_HEADS, d).transpose(0, 2, 1, 3)

    q = to_heads(h[..., OFF_Q:OFF_K], GLA_DK) * (GLA_DK ** -0.5)
    k = to_heads(h[..., OFF_K:OFF_V], GLA_DK)
    v = to_heads(h[..., OFF_V:OFF_R], GLA_DV)
    r = h[..., OFF_R:OFF_AF]
    log_af = to_heads(jax.nn.log_sigmoid((h[..., OFF_AF:OFF_AB] @ w_af2 + b_af).astype(f32)) / GLA_GATE_NORMALIZER, GLA_DK)
    log_ab = to_heads(jax.nn.log_sigmoid((h[..., OFF_AB:OFF_F] @ w_ab2 + b_ab).astype(f32)) / GLA_GATE_NORMALIZER, GLA_DK)
    o_fwd = gla_one_direction(q, k, v, log_af, strict=False)
    rev = lambda t: jnp.flip(t, axis=2)
    o_bwd = rev(gla_one_direction(rev(q), rev(k), rev(v), rev(log_ab), strict=True))
    o = o_fwd + o_bwd
    o = o * lax.rsqrt(jnp.mean(jnp.square(o), axis=-1, keepdims=True) + RMS_EPS) * gla_norm_g.astype(f32)
    o = o.transpose(0, 2, 1, 3).reshape(bsz, seq, V_WIDTH).astype(x.dtype) * jax.nn.silu(r)
    branch_gla = o @ w_ao
    fg = h[..., OFF_F:IN_COLS].reshape(bsz, seq, FNET_GROUPS, FNET_GROUP_DIM).astype(f32)
    fr = jnp.real(jnp.fft.fft2(fg, axes=(1, 3), norm='ortho'))
    branch_fnet = fr.reshape(bsz, seq, F_WIDTH).astype(x.dtype) @ w_fo
    gates = jax.nn.sigmoid(x @ w_mg + b_mg)
    merged = gates[..., :D_MODEL] * branch_gla + gates[..., D_MODEL:] * branch_fnet
    return merged @ w_o


def peer_token_block(xb, w_pq, peer_keys, peer_u, peer_v):
    t = xb.shape[0]
    f32 = jnp.float32
    qh = (xb @ w_pq).reshape(t, PEER_HEADS, 2, PEER_QDIM // 2).astype(f32)
    s = jnp.einsum('thpc,hpkc->thpk', qh, peer_keys.astype(f32))
    s_top, i_top = lax.top_k(s, PEER_TOPK)
    cand = s_top[:, :, 0, :, None] + s_top[:, :, 1, None, :]
    cand_idx = i_top[:, :, 0, :, None] * PEER_NKEYS + i_top[:, :, 1, None, :]
    best, pos = lax.top_k(cand.reshape(t, PEER_HEADS, PEER_TOPK * PEER_TOPK), PEER_TOPK)
    idx = jnp.take_along_axis(cand_idx.reshape(t, PEER_HEADS, PEER_TOPK * PEER_TOPK), pos, axis=-1)
    g = jax.nn.softmax(best, axis=-1).reshape(t, PEER_HEADS * PEER_TOPK)
    idx = idx.reshape(t, PEER_HEADS * PEER_TOPK)
    u = peer_u[idx]
    a = jax.nn.gelu(jnp.einsum('td,ted->te', xb, u).astype(f32), approximate=False)
    w = (g * a).astype(xb.dtype)
    return jnp.einsum('te,ted->td', w, peer_v[idx])


def channel_mixer(x, w_pq, peer_keys, peer_u, peer_v):
    bsz, seq, d = x.shape
    xb = x.reshape(-1, PEER_TOKEN_BLOCK, d)
    out = lax.map(lambda blk: peer_token_block(blk, w_pq, peer_keys, peer_u, peer_v), xb)
    return out.reshape(bsz, seq, d)


def encoder_layer(x, p, w_in, w_af2, b_af, w_ab2, b_ab, gla_norm_g, w_ao, w_fo, w_mg, b_mg, w_o,
                  ln1_g, ln1_b, w_pq, peer_keys, peer_u, peer_v, w_pg, w_pe, ln2_g, ln2_b):
    mix = token_mixer(x, w_in, w_af2, b_af, w_ab2, b_ab, gla_norm_g, w_ao, w_fo, w_mg, b_mg, w_o)
    h = layer_norm(DEEPNORM_ALPHA * x + mix, ln1_g, ln1_b)
    ffn = channel_mixer(h, w_pq, peer_keys, peer_u, peer_v)
    ple = jax.nn.sigmoid(h @ w_pg) * (p.astype(h.dtype) @ w_pe)
    return layer_norm(DEEPNORM_ALPHA * h + ffn + ple, ln2_g, ln2_b)


def trunk(x, p, w_in, w_af2, b_af, w_ab2, b_ab, gla_norm_g, w_ao, w_fo, w_mg, b_mg, w_o,
          ln1_g, ln1_b, w_pq, peer_keys, peer_u, peer_v, w_pg, w_pe, ln2_g, ln2_b):
    for i in range(DEPTH):
        x = encoder_layer(x, p[i], w_in[i], w_af2[i], b_af[i], w_ab2[i], b_ab[i], gla_norm_g[i],
                          w_ao[i], w_fo[i], w_mg[i], b_mg[i], w_o[i], ln1_g[i], ln1_b[i],
                          w_pq[i], peer_keys[i], peer_u[i], peer_v[i], w_pg[i], w_pe[i],
                          ln2_g[i], ln2_b[i])
    return x


def setup_inputs(seed: int = 0) -> dict:
    key = jax.random.key(seed)
    ks = jax.random.split(key, 27)
    f32 = jnp.float32
    nrm = lambda k, shape, scale: jax.random.normal(k, shape, f32) * scale
    d = D_MODEL
    return {
        'x_prompt': nrm(ks[0], (BATCH, SEQ, d), 1.0),
        'x_sample': nrm(ks[1], (DEC_BATCH, DEC_SEQ, d), 1.0),
        'p_prompt': nrm(ks[2], (DEPTH, BATCH, SEQ, PLE_DIM), 1.0),
        'p_sample': nrm(ks[3], (DEPTH, DEC_BATCH, DEC_SEQ, PLE_DIM), 1.0),
        'w_in': nrm(ks[4], (DEPTH, d, IN_COLS), d ** -0.5),
        'w_af2': nrm(ks[5], (DEPTH, GLA_GATE_RANK, QK_WIDTH), GLA_GATE_RANK ** -0.5),
        'b_af': nrm(ks[6], (DEPTH, QK_WIDTH), 0.1),
        'w_ab2': nrm(ks[7], (DEPTH, GLA_GATE_RANK, QK_WIDTH), GLA_GATE_RANK ** -0.5),
        'b_ab': nrm(ks[8], (DEPTH, QK_WIDTH), 0.1),
        'gla_norm_g': 1.0 + nrm(ks[9], (DEPTH, GLA_DV), 0.02),
        'w_ao': nrm(ks[10], (DEPTH, V_WIDTH, d), V_WIDTH ** -0.5),
        'w_fo': nrm(ks[11], (DEPTH, F_WIDTH, d), F_WIDTH ** -0.5),
        'w_mg': nrm(ks[12], (DEPTH, d, 2 * d), d ** -0.5),
        'b_mg': nrm(ks[13], (DEPTH, 2 * d), 0.02),
        'w_o': nrm(ks[14], (DEPTH, d, d), DEEPNORM_BETA * d ** -0.5),
        'ln1_g': 1.0 + nrm(ks[15], (DEPTH, d), 0.02),
        'ln1_b': nrm(ks[16], (DEPTH, d), 0.02),
        'w_pq': nrm(ks[17], (DEPTH, d, PEER_HEADS * PEER_QDIM), d ** -0.5),
        'peer_keys': nrm(ks[18], (DEPTH, PEER_HEADS, 2, PEER_NKEYS, PEER_QDIM // 2), (PEER_QDIM // 2) ** -0.5),
        'peer_u': nrm(ks[19], (DEPTH, PEER_EXPERTS, d), d ** -0.5),
        'peer_v': nrm(ks[20], (DEPTH, PEER_EXPERTS, d), DEEPNORM_BETA),
        'w_pg': nrm(ks[21], (DEPTH, d, d), d ** -0.5),
        'w_pe': nrm(ks[22], (DEPTH, PLE_DIM, d), DEEPNORM_BETA * PLE_DIM ** -0.5),
        'ln2_g': 1.0 + nrm(ks[23], (DEPTH, d), 0.02),
        'ln2_b': nrm(ks[24], (DEPTH, d), 0.02),
    }


def reference(x_prompt, x_sample, p_prompt, p_sample, w_in, w_af2, b_af, w_ab2, b_ab, gla_norm_g,
              w_ao, w_fo, w_mg, b_mg, w_o, ln1_g, ln1_b, w_pq, peer_keys, peer_u, peer_v,
              w_pg, w_pe, ln2_g, ln2_b):
    y_prompt = trunk(x_prompt, p_prompt, w_in, w_af2, b_af, w_ab2, b_ab, gla_norm_g, w_ao, w_fo,
                     w_mg, b_mg, w_o, ln1_g, ln1_b, w_pq, peer_keys, peer_u, peer_v, w_pg, w_pe,
                     ln2_g, ln2_b)
    y_sample = trunk(x_sample, p_sample, w_in, w_af2, b_af, w_ab2, b_ab, gla_norm_g, w_ao, w_fo,
                     w_mg, b_mg, w_o, ln1_g, ln1_b, w_pq, peer_keys, peer_u, peer_v, w_pg, w_pe,
                     ln2_g, ln2_b)
    return (y_prompt, y_sample)
```

```python
import functools
import math

import jax
import jax.numpy as jnp
from jax import lax
from jax.experimental import pallas as pl
from jax.experimental.pallas import tpu as pltpu

F32 = jnp.float32
BF16 = jnp.bfloat16

D_MODEL = 1024
DEPTH = 1
GLA_HEADS = 4
GLA_DK = 64
GLA_DV = 128
GLA_GATE_RANK = 16
GLA_GATE_NORMALIZER = 16.0
GLA_CHUNK = 64
FNET_GROUPS = 4
FNET_GROUP_DIM = 128
PEER_HEADS = 8
PEER_NKEYS = 128
PEER_EXPERTS = PEER_NKEYS * PEER_NKEYS
PEER_QDIM = 256
PEER_TOPK = 16
PLE_DIM = 256
LN_EPS = 1e-5
RMS_EPS = 1e-6
DEEPNORM_ALPHA = (2.0 * DEPTH) ** 0.25

QK_WIDTH = GLA_HEADS * GLA_DK
V_WIDTH = GLA_HEADS * GLA_DV
F_WIDTH = FNET_GROUPS * FNET_GROUP_DIM
OFF_AF = 2 * QK_WIDTH + 2 * V_WIDTH
OFF_F = OFF_AF + 2 * GLA_GATE_RANK

VMEM_LIMIT_BYTES = 56 * 1024 * 1024
LANES = 128

NEG_INF = float("-inf")


def _cparams(sem):
    return pltpu.CompilerParams(dimension_semantics=sem, vmem_limit_bytes=VMEM_LIMIT_BYTES)


def _full(shape):
    nd = len(shape)
    return pl.BlockSpec(shape, lambda *_: (0,) * nd)


def _dot(a, b):
    return jnp.dot(a, b, preferred_element_type=F32)


def _dot_nt(a, b):
    return lax.dot_general(a, b, (((1,), (1,)), ((), ())), preferred_element_type=F32)


def _dot_tn(a, b):
    return lax.dot_general(a, b, (((0,), (0,)), ((), ())), preferred_element_type=F32)


def _layer_norm(x, g, b):
    mu = jnp.mean(x, axis=-1, keepdims=True)
    xc = x - mu
    var = jnp.mean(xc * xc, axis=-1, keepdims=True)
    return xc * lax.rsqrt(var + LN_EPS) * g + b


def _inproj_kernel(x_ref, wq_ref, wg_ref, wf_ref, w2_ref, b2_ref, qkvr_ref, la_ref, f_ref):
    xb = x_ref[...].astype(BF16)
    qkvr_ref[...] = _dot(xb, wq_ref[...])
    g = _dot(xb, wg_ref[...])
    z = _dot(g.astype(BF16), w2_ref[...]) + b2_ref[...]
    log_sig = jnp.minimum(z, 0.0) - jnp.log1p(jnp.exp(-jnp.abs(z)))
    la_ref[...] = log_sig * (1.0 / GLA_GATE_NORMALIZER)
    f_ref[...] = _dot(xb, wf_ref[...]).astype(BF16)


def _inproj(x2, wq, wg, wf, w2, b2, tb):
    t = x2.shape[0]
    nq = wq.shape[1]
    return pl.pallas_call(
        _inproj_kernel,
        grid=(t // tb,),
        in_specs=[
            pl.BlockSpec((tb, D_MODEL), lambda i: (i, 0)),
            _full(wq.shape), _full(wg.shape), _full(wf.shape), _full(w2.shape), _full(b2.shape),
        ],
        out_specs=[
            pl.BlockSpec((tb, nq), lambda i: (i, 0)),
            pl.BlockSpec((tb, 2 * QK_WIDTH), lambda i: (i, 0)),
            pl.BlockSpec((tb, F_WIDTH), lambda i: (i, 0)),
        ],
        out_shape=[
            jax.ShapeDtypeStruct((t, nq), F32),
            jax.ShapeDtypeStruct((t, 2 * QK_WIDTH), F32),
            jax.ShapeDtypeStruct((t, F_WIDTH), BF16),
        ],
        compiler_params=_cparams(("parallel",)),
        name="inproj",
    )(x2, wq, wg, wf, w2, b2)


def _gla_chunk(q, k, v, la, tri, smask, hmask, bmask, st_ref, mid, last):
    c = GLA_CHUNK
    la_hi = la.astype(BF16)
    la_lo = (la - la_hi.astype(F32)).astype(BF16)
    bc = _dot(tri, la_hi) + _dot(tri, la_lo)
    b_mid = bc[mid:mid + 1, :]
    b_last = bc[last:last + 1, :]
    qs = q * (GLA_DK ** -0.5)
    vb = v.astype(BF16)
    qd = qs * jnp.exp(bc - b_mid)
    kd = (k * jnp.exp(b_mid - bc)).astype(BF16)
    qstack = jnp.concatenate(
        [jnp.where(hmask[h:h + 1, :] > 0.0, qd, 0.0) for h in range(GLA_HEADS)], axis=0).astype(BF16)
    scores = _dot_nt(qstack, kd)
    scores = jnp.where(smask > 0.0, scores, 0.0).astype(BF16)
    oi_all = _dot(scores, vb)
    o_intra = jnp.concatenate(
        [oi_all[h * c:(h + 1) * c, h * GLA_DV:(h + 1) * GLA_DV] for h in range(GLA_HEADS)], axis=1)
    q_in = (qs * jnp.exp(bc)).astype(BF16)
    st = st_ref[...]
    o_inter = _dot_nt(q_in, st.astype(BF16))
    k_st = (k * jnp.exp(b_last - bc)).astype(BF16)
    upd = _dot_tn(vb, k_st)
    st_ref[...] = st * jnp.exp(b_last) + jnp.where(bmask > 0.0, upd, 0.0)
    return o_intra + o_inter


def _gla_kernel(qf_ref, kf_ref, vf_ref, laf_ref, qb_ref, kb_ref, vb_ref, lab_ref,
                trif_ref, trib_ref, smf_ref, smb_ref, hm_ref, bm_ref,
                of_ref, ob_ref, stf_ref, stb_ref, *, nchunk):
    @pl.when(pl.program_id(1) == 0)
    def _():
        stf_ref[...] = jnp.zeros_like(stf_ref)
        stb_ref[...] = jnp.zeros_like(stb_ref)

    c = GLA_CHUNK
    trif = trif_ref[...]
    trib = trib_ref[...]
    smf = smf_ref[...]
    smb = smb_ref[...]
    hm = hm_ref[...]
    bm = bm_ref[...]

    def body(j, carry):
        rf = pl.ds(pl.multiple_of(j * c, c), c)
        of_ref[rf, :] = _gla_chunk(qf_ref[rf, :], kf_ref[rf, :], vf_ref[rf, :], laf_ref[rf, :],
                                   trif, smf, hm, bm, stf_ref, c // 2 - 1, c - 1)
        rb = pl.ds(pl.multiple_of((nchunk - 1 - j) * c, c), c)
        ob_ref[rb, :] = _gla_chunk(qb_ref[rb, :], kb_ref[rb, :], vb_ref[rb, :], lab_ref[rb, :],
                                   trib, smb, hm, bm, stb_ref, c // 2, 0)
        return carry

    lax.fori_loop(0, nchunk, body, 0)


def _gla_consts():
    c = GLA_CHUNK
    r = jnp.arange(c)
    trif = (r[:, None] >= r[None, :]).astype(BF16)
    trib = (r[None, :] >= r[:, None]).astype(BF16)
    rr = jnp.arange(GLA_HEADS * c) % c
    smf = (rr[:, None] >= r[None, :]).astype(F32)
    smb = (r[None, :] > rr[:, None]).astype(F32)
    hm = (jnp.arange(QK_WIDTH)[None, :] // GLA_DK == jnp.arange(GLA_HEADS)[:, None]).astype(F32)
    hm = jnp.concatenate([hm, jnp.zeros((8 - GLA_HEADS, QK_WIDTH), F32)], axis=0)
    bm = (jnp.arange(V_WIDTH)[:, None] // GLA_DV == jnp.arange(QK_WIDTH)[None, :] // GLA_DK).astype(F32)
    return trif, trib, smf, smb, hm, bm


def _gla(qkvr, la, bsz, seq, tbg):
    t = bsz * seq
    nblk = seq // tbg
    nchunk = tbg // GLA_CHUNK
    consts = _gla_consts()
    fwd = lambda col: (lambda b, i: (b * nblk + i, col))
    bwd = lambda col: (lambda b, i: (b * nblk + nblk - 1 - i, col))
    in_specs = [
        pl.BlockSpec((tbg, QK_WIDTH), fwd(0)),
        pl.BlockSpec((tbg, QK_WIDTH), fwd(1)),
        pl.BlockSpec((tbg, V_WIDTH), fwd(1)),
        pl.BlockSpec((tbg, QK_WIDTH), fwd(0)),
        pl.BlockSpec((tbg, QK_WIDTH), bwd(0)),
        pl.BlockSpec((tbg, QK_WIDTH), bwd(1)),
        pl.BlockSpec((tbg, V_WIDTH), bwd(1)),
        pl.BlockSpec((tbg, QK_WIDTH), bwd(1)),
    ] + [_full(a.shape) for a in consts]
    return pl.pallas_call(
        functools.partial(_gla_kernel, nchunk=nchunk),
        grid=(bsz, nblk),
        in_specs=in_specs,
        out_specs=[pl.BlockSpec((tbg, V_WIDTH), fwd(0)), pl.BlockSpec((tbg, V_WIDTH), bwd(0))],
        out_shape=[jax.ShapeDtypeStruct((t, V_WIDTH), F32), jax.ShapeDtypeStruct((t, V_WIDTH), F32)],
        scratch_shapes=[pltpu.VMEM((V_WIDTH, QK_WIDTH), F32), pltpu.VMEM((V_WIDTH, QK_WIDTH), F32)],
        compiler_params=_cparams(("parallel", "arbitrary")),
        name="gla",
    )(qkvr, qkvr, qkvr, la, qkvr, qkvr, qkvr, la, *consts)


def _dft_kernel(c_ref, s_ref, f_ref, c128_ref, s128_ref, o_ref, p_acc, q_acc):
    k = pl.program_id(2)

    @pl.when(k == 0)
    def _():
        p_acc[...] = jnp.zeros_like(p_acc)
        q_acc[...] = jnp.zeros_like(q_acc)

    fb = f_ref[...]
    p_acc[...] += _dot(c_ref[...], fb)
    q_acc[...] += _dot(s_ref[...], fb)

    @pl.when(k == pl.num_programs(2) - 1)
    def _():
        c128 = c128_ref[...]
        s128 = s128_ref[...]
        for g in range(FNET_GROUPS):
            sl = slice(g * FNET_GROUP_DIM, (g + 1) * FNET_GROUP_DIM)
            pg = p_acc[:, sl].astype(BF16)
            qg = q_acc[:, sl].astype(BF16)
            o_ref[:, sl] = (_dot(pg, c128) - _dot(qg, s128)).astype(BF16)


def _dft_tables(n, split):
    k = jnp.arange(n, dtype=jnp.int32)
    m_hi = jnp.arange(n // split, dtype=jnp.int32) * split
    m_lo = jnp.arange(split, dtype=jnp.int32)
    w = 2.0 * math.pi / n
    ang_hi = ((m_hi[:, None] * k[None, :]) % n).astype(F32) * w
    ang_lo = ((m_lo[:, None] * k[None, :]) % n).astype(F32) * w
    ch, sh = jnp.cos(ang_hi)[:, None, :], jnp.sin(ang_hi)[:, None, :]
    cl, sl = jnp.cos(ang_lo)[None, :, :], jnp.sin(ang_lo)[None, :, :]
    c = (ch * cl - sh * sl).reshape(n, n)
    s = (sh * cl + ch * sl).reshape(n, n)
    return c.astype(BF16), s.astype(BF16)


def _small_dft_tables(n, scale):
    k = jnp.arange(n, dtype=jnp.int32)
    ang = ((k[:, None] * k[None, :]) % n).astype(F32) * (2.0 * math.pi / n)
    return (jnp.cos(ang) * scale).astype(BF16), (jnp.sin(ang) * scale).astype(BF16)


def _fnet(f, bsz, seq, tm, tk):
    t = bsz * seq
    c_l, s_l = _dft_tables(seq, 64)
    c128, s128 = _small_dft_tables(FNET_GROUP_DIM, (seq * FNET_GROUP_DIM) ** -0.5)
    nm, nk = seq // tm, seq // tk
    return pl.pallas_call(
        _dft_kernel,
        grid=(bsz, nm, nk),
        in_specs=[
            pl.BlockSpec((tm, tk), lambda b, m, k: (m, k)),
            pl.BlockSpec((tm, tk), lambda b, m, k: (m, k)),
            pl.BlockSpec((tk, F_WIDTH), lambda b, m, k: (b * nk + k, 0)),
            _full(c128.shape), _full(s128.shape),
        ],
        out_specs=pl.BlockSpec((tm, F_WIDTH), lambda b, m, k: (b * nm + m, 0)),
        out_shape=jax.ShapeDtypeStruct((t, F_WIDTH), BF16),
        scratch_shapes=[pltpu.VMEM((tm, F_WIDTH), F32), pltpu.VMEM((tm, F_WIDTH), F32)],
        compiler_params=_cparams(("parallel", "parallel", "arbitrary")),
        name="fnet_dft",
    )(c_l, s_l, f, c128, s128)


def _merge_kernel(x_ref, of_ref, ob_ref, r_ref, fr_ref, p_ref,
                  gn_ref, wao_ref, wfo_ref, wmg_ref, bmg_ref, wo_ref, l1g_ref, l1b_ref,
                  wpg_ref, wpe_ref, hb_ref, r2_ref):
    x = x_ref[...]
    xb = x.astype(BF16)
    o = of_ref[...] + ob_ref[...]
    parts = []
    for h in range(GLA_HEADS):
        oh = o[:, h * GLA_DV:(h + 1) * GLA_DV]
        ms = jnp.mean(oh * oh, axis=-1, keepdims=True)
        parts.append(oh * lax.rsqrt(ms + RMS_EPS) * gn_ref[...])
    on = jnp.concatenate(parts, axis=1)
    r = r_ref[...]
    og = (on * (r * jax.nn.sigmoid(r))).astype(BF16)
    branch_gla = _dot(og, wao_ref[...])
    branch_fnet = _dot(fr_ref[...], wfo_ref[...])
    gates = jax.nn.sigmoid(_dot(xb, wmg_ref[...]) + bmg_ref[...])
    merged = gates[:, :D_MODEL] * branch_gla + gates[:, D_MODEL:] * branch_fnet
    mix = _dot(merged.astype(BF16), wo_ref[...])
    h1 = _layer_norm(DEEPNORM_ALPHA * x + mix, l1g_ref[...], l1b_ref[...])
    h1b = h1.astype(BF16)
    hb_ref[...] = h1b
    ple = jax.nn.sigmoid(_dot(h1b, wpg_ref[...])) * _dot(p_ref[...].astype(BF16), wpe_ref[...])
    r2_ref[...] = DEEPNORM_ALPHA * h1 + ple


def _merge(x2, o_f, o_b, qkvr, fr, p2, gn, wao, wfo, wmg, bmg, wo, l1g, l1b, wpg, wpe, tb):
    t = x2.shape[0]
    row = lambda w: pl.BlockSpec((tb, w), lambda i: (i, 0))
    weights = (gn, wao, wfo, wmg, bmg, wo, l1g, l1b, wpg, wpe)
    return pl.pallas_call(
        _merge_kernel,
        grid=(t // tb,),
        in_specs=[row(D_MODEL), row(V_WIDTH), row(V_WIDTH),
                  pl.BlockSpec((tb, V_WIDTH), lambda i: (i, 2)),
                  row(F_WIDTH), row(PLE_DIM)] + [_full(w.shape) for w in weights],
        out_specs=[row(D_MODEL), row(D_MODEL)],
        out_shape=[jax.ShapeDtypeStruct((t, D_MODEL), BF16), jax.ShapeDtypeStruct((t, D_MODEL), F32)],
        compiler_params=_cparams(("parallel",)),
        name="merge_ln1",
    )(x2, o_f, o_b, qkvr, fr, p2, *weights)


def _top16_desc(s):
    vals = []
    cur = s
    for _ in range(PEER_TOPK):
        m = jnp.max(cur, axis=0, keepdims=True)
        vals.append(m)
        cur = jnp.where(cur == m, NEG_INF, cur)
    return vals


def _route_kernel(hb_ref, wpq_ref, keys_ref, s2_ref, e2_ref, tau_ref, c1_ref):
    qh = _dot(hb_ref[...], wpq_ref[...]).astype(BF16)
    nk = PEER_NKEYS
    for h in range(PEER_HEADS):
        q1 = qh[:, (2 * h) * nk:(2 * h + 1) * nk]
        q2 = qh[:, (2 * h + 1) * nk:(2 * h + 2) * nk]
        s1 = _dot_nt(keys_ref[2 * h], q1)
        s2 = _dot_nt(keys_ref[2 * h + 1], q2)
        a = _top16_desc(s1)
        b = _top16_desc(s2)
        cands = []
        for k in range(PEER_TOPK):
            for l in range(PEER_TOPK // (k + 1)):
                cands.append(a[k] + b[l])
        pad = (-len(cands)) % 8
        cand = jnp.concatenate(cands + [jnp.full_like(a[0], NEG_INF)] * pad, axis=0)
        cur = cand
        for _ in range(PEER_TOPK - 1):
            m = jnp.max(cur, axis=0, keepdims=True)
            cur = jnp.where(cur == m, NEG_INF, cur)
        theta = jnp.max(cur, axis=0, keepdims=True)
        top = a[0] + b[0]
        z = jnp.sum(jnp.where(cand >= theta, jnp.exp(cand - top), 0.0), axis=0, keepdims=True)
        rows = slice(h * nk, (h + 1) * nk)
        s2_ref[0, rows, :] = s2
        e2_ref[0, rows, :] = jnp.exp(s2 - b[0])
        tau_ref[0, rows, :] = theta - s1
        c1_ref[0, rows, :] = jnp.exp(s1 - a[0]) / z


def _route(hb, wpq, keys, tb):
    t = hb.shape[0]
    nblk = t // tb
    rows = PEER_HEADS * PEER_NKEYS
    out_spec = pl.BlockSpec((1, rows, tb), lambda i: (i, 0, 0))
    out_shape = jax.ShapeDtypeStruct((nblk, rows, tb), F32)
    return pl.pallas_call(
        _route_kernel,
        grid=(nblk,),
        in_specs=[pl.BlockSpec((tb, D_MODEL), lambda i: (i, 0)), _full(wpq.shape), _full(keys.shape)],
        out_specs=[out_spec] * 4,
        out_shape=[out_shape] * 4,
        compiler_params=_cparams(("parallel",)),
        name="peer_route",
    )(hb, wpq, keys)


def _peer_kernel(hb_ref, s2_ref, e2_ref, tau_ref, c1_ref, u_ref, vt_ref, r2_ref, l2g_ref, l2b_ref,
                 y_ref, acc_ref, a_ref, p_ref, *, ec, tb):
    c = pl.program_id(1)
    nk = PEER_NKEYS

    @pl.when(c == 0)
    def _():
        acc_ref[...] = jnp.zeros_like(acc_ref)

    a_ref[...] = _dot_nt(u_ref[...], hb_ref[...])
    assert ec // nk == 8
    for tt in range(tb // LANES):
        cols = slice(tt * LANES, (tt + 1) * LANES)
        tau8 = [tau_ref[0, pl.ds(pl.multiple_of(h * nk + c * 8, 8), 8), cols] for h in range(PEER_HEADS)]
        c18 = [c1_ref[0, pl.ds(pl.multiple_of(h * nk + c * 8, 8), 8), cols] for h in range(PEER_HEADS)]
        for ii in range(8):
            w = jnp.zeros((nk, LANES), F32)
            for h in range(PEER_HEADS):
                rows = slice(h * nk, (h + 1) * nk)
                tau = tau8[h][ii:ii + 1, :]
                c1 = c18[h][ii:ii + 1, :]
                w = w + c1 * jnp.where(s2_ref[0, rows, cols] >= tau, e2_ref[0, rows, cols], 0.0)
            av = a_ref[ii * nk:(ii + 1) * nk, cols]
            gelu = 0.5 * av * (1.0 + lax.erf(av * (2.0 ** -0.5)))
            p_ref[ii * nk:(ii + 1) * nk, cols] = (gelu * w).astype(BF16)
    acc_ref[...] += _dot(vt_ref[...], p_ref[...])

    @pl.when(c == pl.num_programs(1) - 1)
    def _():
        ffn = acc_ref[...].T
        y_ref[...] = _layer_norm(r2_ref[...] + ffn, l2g_ref[...], l2b_ref[...])


def _peer(hb, s2, e2, tau, c1, u_b, vt_b, r2, l2g, l2b, tb, ec):
    t = hb.shape[0]
    nblk = t // tb
    rows = PEER_HEADS * PEER_NKEYS
    route_spec = pl.BlockSpec((1, rows, tb), lambda i, c: (i, 0, 0))
    return pl.pallas_call(
        functools.partial(_peer_kernel, ec=ec, tb=tb),
        grid=(nblk, PEER_EXPERTS // ec),
        in_specs=[pl.BlockSpec((tb, D_MODEL), lambda i, c: (i, 0)),
                  route_spec, route_spec, route_spec, route_spec,
                  pl.BlockSpec((ec, D_MODEL), lambda i, c: (c, 0)),
                  pl.BlockSpec((D_MODEL, ec), lambda i, c: (0, c)),
                  pl.BlockSpec((tb, D_MODEL), lambda i, c: (i, 0)),
                  _full(l2g.shape), _full(l2b.shape)],
        out_specs=pl.BlockSpec((tb, D_MODEL), lambda i, c: (i, 0)),
        out_shape=jax.ShapeDtypeStruct((t, D_MODEL), F32),
        scratch_shapes=[pltpu.VMEM((D_MODEL, tb), F32), pltpu.VMEM((ec, tb), F32), pltpu.VMEM((ec, tb), BF16)],
        compiler_params=_cparams(("parallel", "arbitrary")),
        name="peer_experts",
    )(hb, s2, e2, tau, c1, u_b, vt_b, r2, l2g, l2b)


def _block_sizes(bsz, seq):
    t = bsz * seq
    return dict(
        tb_in=min(512, t),
        tbg=min(512, seq),
        tm=min(1024, seq), tk=min(1024, seq),
        tb_merge=min(256, t),
        tb_peer=min(512, t),
        ec=8 * PEER_NKEYS,
    )


def _prep_weights(w_in, w_af2, b_af, w_ab2, b_ab, gla_norm_g, w_ao, w_fo, w_mg, b_mg, w_o,
                  ln1_g, ln1_b, w_pq, peer_keys, peer_u, peer_v, w_pg, w_pe, ln2_g, ln2_b):
    r = GLA_GATE_RANK
    zeros = jnp.zeros((r, QK_WIDTH), F32)
    w2 = jnp.concatenate([jnp.concatenate([w_af2, zeros], axis=1),
                          jnp.concatenate([zeros, w_ab2], axis=1)], axis=0)
    return dict(
        wq=w_in[:, :OFF_AF].astype(BF16),
        wg=w_in[:, OFF_AF:OFF_F].astype(BF16),
        wf=w_in[:, OFF_F:].astype(BF16),
        w2=w2.astype(BF16),
        b2=jnp.concatenate([b_af, b_ab])[None, :],
        gn=gla_norm_g[None, :],
        wao=w_ao.astype(BF16), wfo=w_fo.astype(BF16), wmg=w_mg.astype(BF16), bmg=b_mg[None, :],
        wo=w_o.astype(BF16), l1g=ln1_g[None, :], l1b=ln1_b[None, :],
        wpq=w_pq.astype(BF16),
        keys=peer_keys.reshape(PEER_HEADS * 2, PEER_NKEYS, PEER_QDIM // 2).astype(BF16),
        u=peer_u.astype(BF16), vt=peer_v.astype(BF16).T,
        wpg=w_pg.astype(BF16), wpe=w_pe.astype(BF16), l2g=ln2_g[None, :], l2b=ln2_b[None, :],
    )


def _encoder_layer(x, p, w):
    bsz, seq, d = x.shape
    t = bsz * seq
    bs = _block_sizes(bsz, seq)
    x2 = x.reshape(t, d)
    p2 = p.reshape(t, PLE_DIM)
    qkvr, la, f = _inproj(x2, w["wq"], w["wg"], w["wf"], w["w2"], w["b2"], bs["tb_in"])
    o_f, o_b = _gla(qkvr, la, bsz, seq, bs["tbg"])
    fr = _fnet(f, bsz, seq, bs["tm"], bs["tk"])
    hb, r2 = _merge(x2, o_f, o_b, qkvr, fr, p2, w["gn"], w["wao"], w["wfo"], w["wmg"], w["bmg"],
                    w["wo"], w["l1g"], w["l1b"], w["wpg"], w["wpe"], bs["tb_merge"])
    tbp = bs["tb_peer"]
    s2, e2, tau, c1 = _route(hb, w["wpq"], w["keys"], tbp)
    y = _peer(hb, s2, e2, tau, c1, w["u"], w["vt"], r2, w["l2g"], w["l2b"], tbp, bs["ec"])
    return y.reshape(bsz, seq, d)


def kernel(x_prompt, x_sample, p_prompt, p_sample, w_in, w_af2, b_af, w_ab2, b_ab, gla_norm_g, w_ao, w_fo, w_mg, b_mg, w_o, ln1_g, ln1_b, w_pq, peer_keys, peer_u, peer_v, w_pg, w_pe, ln2_g, ln2_b):
    params = (w_in, w_af2, b_af, w_ab2, b_ab, gla_norm_g, w_ao, w_fo, w_mg, b_mg, w_o,
              ln1_g, ln1_b, w_pq, peer_keys, peer_u, peer_v, w_pg, w_pe, ln2_g, ln2_b)
    xs = (x_prompt, x_sample)
    ps = (p_prompt, p_sample)
    for i in range(DEPTH):
        w = _prep_weights(*[a[i] for a in params])
        xs = tuple(_encoder_layer(x, p[i], w) for x, p in zip(xs, ps))
    return xs
```

```python
import functools
import math

import jax
import jax.numpy as jnp
from jax import lax
from jax.experimental import pallas as pl
from jax.experimental.pallas import tpu as pltpu

F32 = jnp.float32
BF16 = jnp.bfloat16

D_MODEL = 1024
DEPTH = 1
GLA_HEADS = 4
GLA_DK = 64
GLA_DV = 128
GLA_GATE_RANK = 16
GLA_GATE_NORMALIZER = 16.0
GLA_CHUNK = 64
FNET_GROUPS = 4
FNET_GROUP_DIM = 128
PEER_HEADS = 8
PEER_NKEYS = 128
PEER_EXPERTS = PEER_NKEYS * PEER_NKEYS
PEER_QDIM = 256
PEER_TOPK = 16
PLE_DIM = 256
LN_EPS = 1e-5
RMS_EPS = 1e-6
DEEPNORM_ALPHA = (2.0 * DEPTH) ** 0.25

QK_WIDTH = GLA_HEADS * GLA_DK
V_WIDTH = GLA_HEADS * GLA_DV
F_WIDTH = FNET_GROUPS * FNET_GROUP_DIM
OFF_AF = 2 * QK_WIDTH + 2 * V_WIDTH
OFF_F = OFF_AF + 2 * GLA_GATE_RANK

VMEM_LIMIT_BYTES = 56 * 1024 * 1024
LANES = 128

NEG_INF = float("-inf")


def _cparams(sem, flags=None):
    return pltpu.CompilerParams(dimension_semantics=sem, vmem_limit_bytes=VMEM_LIMIT_BYTES, flags=flags)


def _full(shape):
    nd = len(shape)
    return pl.BlockSpec(shape, lambda *_: (0,) * nd)


def _dot(a, b):
    return jnp.dot(a, b, preferred_element_type=F32)


def _dot_nt(a, b):
    return lax.dot_general(a, b, (((1,), (1,)), ((), ())), preferred_element_type=F32)


def _dot_tn(a, b):
    return lax.dot_general(a, b, (((0,), (0,)), ((), ())), preferred_element_type=F32)


def _layer_norm(x, g, b):
    mu = jnp.mean(x, axis=-1, keepdims=True)
    xc = x - mu
    var = jnp.mean(xc * xc, axis=-1, keepdims=True)
    return xc * lax.rsqrt(var + LN_EPS) * g + b


def _inproj_kernel(x_ref, wq_ref, wg_ref, wf_ref, w2_ref, b2_ref, qkvr_ref, la_ref, f_ref):
    xb = x_ref[...].astype(BF16)
    qkvr_ref[...] = _dot(xb, wq_ref[...])
    g = _dot(xb, wg_ref[...])
    z = _dot(g.astype(BF16), w2_ref[...]) + b2_ref[...]
    log_sig = jnp.minimum(z, 0.0) - jnp.log1p(jnp.exp(-jnp.abs(z)))
    la_ref[...] = log_sig * (1.0 / GLA_GATE_NORMALIZER)
    f_ref[...] = _dot(xb, wf_ref[...]).astype(BF16)


def _inproj(x2, wq, wg, wf, w2, b2, tb):
    t = x2.shape[0]
    nq = wq.shape[1]
    return pl.pallas_call(
        _inproj_kernel,
        grid=(t // tb,),
        in_specs=[
            pl.BlockSpec((tb, D_MODEL), lambda i: (i, 0)),
            _full(wq.shape), _full(wg.shape), _full(wf.shape), _full(w2.shape), _full(b2.shape),
        ],
        out_specs=[
            pl.BlockSpec((tb, nq), lambda i: (i, 0)),
            pl.BlockSpec((tb, 2 * QK_WIDTH), lambda i: (i, 0)),
            pl.BlockSpec((tb, F_WIDTH), lambda i: (i, 0)),
        ],
        out_shape=[
            jax.ShapeDtypeStruct((t, nq), F32),
            jax.ShapeDtypeStruct((t, 2 * QK_WIDTH), F32),
            jax.ShapeDtypeStruct((t, F_WIDTH), BF16),
        ],
        compiler_params=_cparams(("parallel",)),
        name="inproj",
    )(x2, wq, wg, wf, w2, b2)


def _gla_chunk(q, k, v, la, tri, smask, hmask, bmask, st_ref, mid, last):
    c = GLA_CHUNK
    la_hi = la.astype(BF16)
    la_lo = (la - la_hi.astype(F32)).astype(BF16)
    bc = _dot(tri, la_hi) + _dot(tri, la_lo)
    b_mid = bc[mid:mid + 1, :]
    b_last = bc[last:last + 1, :]
    qs = q * (GLA_DK ** -0.5)
    vb = v.astype(BF16)
    qd = qs * jnp.exp(bc - b_mid)
    kd = (k * jnp.exp(b_mid - bc)).astype(BF16)
    qstack = jnp.concatenate(
        [jnp.where(hmask[h:h + 1, :] > 0.0, qd, 0.0) for h in range(GLA_HEADS)], axis=0).astype(BF16)
    scores = _dot_nt(qstack, kd)
    scores = jnp.where(smask > 0.0, scores, 0.0).astype(BF16)
    oi_all = _dot(scores, vb)
    o_intra = jnp.concatenate(
        [oi_all[h * c:(h + 1) * c, h * GLA_DV:(h + 1) * GLA_DV] for h in range(GLA_HEADS)], axis=1)
    q_in = (qs * jnp.exp(bc)).astype(BF16)
    st = st_ref[...]
    o_inter = _dot_nt(q_in, st.astype(BF16))
    k_st = (k * jnp.exp(b_last - bc)).astype(BF16)
    upd = _dot_tn(vb, k_st)
    st_ref[...] = st * jnp.exp(b_last) + jnp.where(bmask > 0.0, upd, 0.0)
    return o_intra + o_inter


def _gla_kernel(qf_ref, kf_ref, vf_ref, laf_ref, qb_ref, kb_ref, vb_ref, lab_ref,
                trif_ref, trib_ref, smf_ref, smb_ref, hm_ref, bm_ref,
                of_ref, ob_ref, stf_ref, stb_ref, *, nchunk):
    @pl.when(pl.program_id(1) == 0)
    def _():
        stf_ref[...] = jnp.zeros_like(stf_ref)
        stb_ref[...] = jnp.zeros_like(stb_ref)

    c = GLA_CHUNK
    trif = trif_ref[...]
    trib = trib_ref[...]
    smf = smf_ref[...]
    smb = smb_ref[...]
    hm = hm_ref[...]
    bm = bm_ref[...]

    def body(j, carry):
        rf = pl.ds(pl.multiple_of(j * c, c), c)
        of_ref[rf, :] = _gla_chunk(qf_ref[rf, :], kf_ref[rf, :], vf_ref[rf, :], laf_ref[rf, :],
                                   trif, smf, hm, bm, stf_ref, c // 2 - 1, c - 1)
        rb = pl.ds(pl.multiple_of((nchunk - 1 - j) * c, c), c)
        ob_ref[rb, :] = _gla_chunk(qb_ref[rb, :], kb_ref[rb, :], vb_ref[rb, :], lab_ref[rb, :],
                                   trib, smb, hm, bm, stb_ref, c // 2, 0)
        return carry

    lax.fori_loop(0, nchunk, body, 0)


def _gla_consts():
    c = GLA_CHUNK
    r = jnp.arange(c)
    trif = (r[:, None] >= r[None, :]).astype(BF16)
    trib = (r[None, :] >= r[:, None]).astype(BF16)
    rr = jnp.arange(GLA_HEADS * c) % c
    smf = (rr[:, None] >= r[None, :]).astype(F32)
    smb = (r[None, :] > rr[:, None]).astype(F32)
    hm = (jnp.arange(QK_WIDTH)[None, :] // GLA_DK == jnp.arange(GLA_HEADS)[:, None]).astype(F32)
    hm = jnp.concatenate([hm, jnp.zeros((8 - GLA_HEADS, QK_WIDTH), F32)], axis=0)
    bm = (jnp.arange(V_WIDTH)[:, None] // GLA_DV == jnp.arange(QK_WIDTH)[None, :] // GLA_DK).astype(F32)
    return trif, trib, smf, smb, hm, bm


def _gla(qkvr, la, bsz, seq, tbg):
    t = bsz * seq
    nblk = seq // tbg
    nchunk = tbg // GLA_CHUNK
    consts = _gla_consts()
    fwd = lambda col: (lambda b, i: (b * nblk + i, col))
    bwd = lambda col: (lambda b, i: (b * nblk + nblk - 1 - i, col))
    in_specs = [
        pl.BlockSpec((tbg, QK_WIDTH), fwd(0)),
        pl.BlockSpec((tbg, QK_WIDTH), fwd(1)),
        pl.BlockSpec((tbg, V_WIDTH), fwd(1)),
        pl.BlockSpec((tbg, QK_WIDTH), fwd(0)),
        pl.BlockSpec((tbg, QK_WIDTH), bwd(0)),
        pl.BlockSpec((tbg, QK_WIDTH), bwd(1)),
        pl.BlockSpec((tbg, V_WIDTH), bwd(1)),
        pl.BlockSpec((tbg, QK_WIDTH), bwd(1)),
    ] + [_full(a.shape) for a in consts]
    return pl.pallas_call(
        functools.partial(_gla_kernel, nchunk=nchunk),
        grid=(bsz, nblk),
        in_specs=in_specs,
        out_specs=[pl.BlockSpec((tbg, V_WIDTH), fwd(0)), pl.BlockSpec((tbg, V_WIDTH), bwd(0))],
        out_shape=[jax.ShapeDtypeStruct((t, V_WIDTH), F32), jax.ShapeDtypeStruct((t, V_WIDTH), F32)],
        scratch_shapes=[pltpu.VMEM((V_WIDTH, QK_WIDTH), F32), pltpu.VMEM((V_WIDTH, QK_WIDTH), F32)],
        compiler_params=_cparams(("parallel", "arbitrary")),
        name="gla",
    )(qkvr, qkvr, qkvr, la, qkvr, qkvr, qkvr, la, *consts)


def _dft_kernel(c_ref, s_ref, f_ref, c128_ref, s128_ref, o_ref, p_acc, q_acc):
    k = pl.program_id(2)

    @pl.when(k == 0)
    def _():
        p_acc[...] = jnp.zeros_like(p_acc)
        q_acc[...] = jnp.zeros_like(q_acc)

    fb = f_ref[...]
    p_acc[...] += _dot(c_ref[...], fb)
    q_acc[...] += _dot(s_ref[...], fb)

    @pl.when(k == pl.num_programs(2) - 1)
    def _():
        c128 = c128_ref[...]
        s128 = s128_ref[...]
        for g in range(FNET_GROUPS):
            sl = slice(g * FNET_GROUP_DIM, (g + 1) * FNET_GROUP_DIM)
            pg = p_acc[:, sl].astype(BF16)
            qg = q_acc[:, sl].astype(BF16)
            o_ref[:, sl] = (_dot(pg, c128) - _dot(qg, s128)).astype(BF16)


def _dft_tables(n, split):
    k = jnp.arange(n, dtype=jnp.int32)
    m_hi = jnp.arange(n // split, dtype=jnp.int32) * split
    m_lo = jnp.arange(split, dtype=jnp.int32)
    w = 2.0 * math.pi / n
    ang_hi = ((m_hi[:, None] * k[None, :]) % n).astype(F32) * w
    ang_lo = ((m_lo[:, None] * k[None, :]) % n).astype(F32) * w
    ch, sh = jnp.cos(ang_hi)[:, None, :], jnp.sin(ang_hi)[:, None, :]
    cl, sl = jnp.cos(ang_lo)[None, :, :], jnp.sin(ang_lo)[None, :, :]
    c = (ch * cl - sh * sl).reshape(n, n)
    s = (sh * cl + ch * sl).reshape(n, n)
    return c.astype(BF16), s.astype(BF16)


def _small_dft_tables(n, scale):
    k = jnp.arange(n, dtype=jnp.int32)
    ang = ((k[:, None] * k[None, :]) % n).astype(F32) * (2.0 * math.pi / n)
    return (jnp.cos(ang) * scale).astype(BF16), (jnp.sin(ang) * scale).astype(BF16)


def _fnet(f, bsz, seq, tm, tk):
    t = bsz * seq
    c_l, s_l = _dft_tables(seq, 64)
    c128, s128 = _small_dft_tables(FNET_GROUP_DIM, (seq * FNET_GROUP_DIM) ** -0.5)
    nm, nk = seq // tm, seq // tk
    return pl.pallas_call(
        _dft_kernel,
        grid=(bsz, nm, nk),
        in_specs=[
            pl.BlockSpec((tm, tk), lambda b, m, k: (m, k)),
            pl.BlockSpec((tm, tk), lambda b, m, k: (m, k)),
            pl.BlockSpec((tk, F_WIDTH), lambda b, m, k: (b * nk + k, 0)),
            _full(c128.shape), _full(s128.shape),
        ],
        out_specs=pl.BlockSpec((tm, F_WIDTH), lambda b, m, k: (b * nm + m, 0)),
        out_shape=jax.ShapeDtypeStruct((t, F_WIDTH), BF16),
        scratch_shapes=[pltpu.VMEM((tm, F_WIDTH), F32), pltpu.VMEM((tm, F_WIDTH), F32)],
        compiler_params=_cparams(("parallel", "parallel", "arbitrary")),
        name="fnet_dft",
    )(c_l, s_l, f, c128, s128)


def _merge_kernel(x_ref, of_ref, ob_ref, r_ref, fr_ref, p_ref,
                  gn_ref, wao_ref, wfo_ref, wmg_ref, bmg_ref, wo_ref, l1g_ref, l1b_ref,
                  wpg_ref, wpe_ref, hb_ref, r2_ref):
    x = x_ref[...]
    xb = x.astype(BF16)
    o = of_ref[...] + ob_ref[...]
    parts = []
    for h in range(GLA_HEADS):
        oh = o[:, h * GLA_DV:(h + 1) * GLA_DV]
        ms = jnp.mean(oh * oh, axis=-1, keepdims=True)
        parts.append(oh * lax.rsqrt(ms + RMS_EPS) * gn_ref[...])
    on = jnp.concatenate(parts, axis=1)
    r = r_ref[...]
    og = (on * (r * jax.nn.sigmoid(r))).astype(BF16)
    branch_gla = _dot(og, wao_ref[...])
    branch_fnet = _dot(fr_ref[...], wfo_ref[...])
    gates = jax.nn.sigmoid(_dot(xb, wmg_ref[...]) + bmg_ref[...])
    merged = gates[:, :D_MODEL] * branch_gla + gates[:, D_MODEL:] * branch_fnet
    mix = _dot(merged.astype(BF16), wo_ref[...])
    h1 = _layer_norm(DEEPNORM_ALPHA * x + mix, l1g_ref[...], l1b_ref[...])
    h1b = h1.astype(BF16)
    hb_ref[...] = h1b
    ple = jax.nn.sigmoid(_dot(h1b, wpg_ref[...])) * _dot(p_ref[...].astype(BF16), wpe_ref[...])
    r2_ref[...] = DEEPNORM_ALPHA * h1 + ple


def _merge(x2, o_f, o_b, qkvr, fr, p2, gn, wao, wfo, wmg, bmg, wo, l1g, l1b, wpg, wpe, tb):
    t = x2.shape[0]
    row = lambda w: pl.BlockSpec((tb, w), lambda i: (i, 0))
    weights = (gn, wao, wfo, wmg, bmg, wo, l1g, l1b, wpg, wpe)
    return pl.pallas_call(
        _merge_kernel,
        grid=(t // tb,),
        in_specs=[row(D_MODEL), row(V_WIDTH), row(V_WIDTH),
                  pl.BlockSpec((tb, V_WIDTH), lambda i: (i, 2)),
                  row(F_WIDTH), row(PLE_DIM)] + [_full(w.shape) for w in weights],
        out_specs=[row(D_MODEL), row(D_MODEL)],
        out_shape=[jax.ShapeDtypeStruct((t, D_MODEL), BF16), jax.ShapeDtypeStruct((t, D_MODEL), F32)],
        compiler_params=_cparams(("parallel",)),
        name="merge_ln1",
    )(x2, o_f, o_b, qkvr, fr, p2, *weights)


RANK_OUTSIDE = 64.0


def _top16_desc(s, want_rank=False):
    vals = []
    cur = s
    rank = jnp.full(s.shape, RANK_OUTSIDE, F32) if want_rank else None
    for r in range(PEER_TOPK):
        m = jnp.max(cur, axis=0, keepdims=True)
        vals.append(m)
        hit = cur == m
        if want_rank:
            rank = jnp.where(hit, float(r), rank)
        cur = jnp.where(hit, NEG_INF, cur)
    return (vals, rank) if want_rank else vals


def _route_kernel(hb_ref, wpq_ref, keys_ref, rank_ref, e2_ref, n_ref, c1_ref):
    qh = _dot(hb_ref[...], wpq_ref[...]).astype(BF16)
    nk = PEER_NKEYS
    for h in range(PEER_HEADS):
        q1 = qh[:, (2 * h) * nk:(2 * h + 1) * nk]
        q2 = qh[:, (2 * h + 1) * nk:(2 * h + 2) * nk]
        s1 = _dot_nt(keys_ref[2 * h], q1)
        s2 = _dot_nt(keys_ref[2 * h + 1], q2)
        a = _top16_desc(s1)
        b, rank2 = _top16_desc(s2, want_rank=True)
        cands = []
        for k in range(PEER_TOPK):
            for l in range(PEER_TOPK // (k + 1)):
                cands.append(a[k] + b[l])
        pad = (-len(cands)) % 8
        cand = jnp.concatenate(cands + [jnp.full_like(a[0], NEG_INF)] * pad, axis=0)
        cur = cand
        for _ in range(PEER_TOPK - 1):
            m = jnp.max(cur, axis=0, keepdims=True)
            cur = jnp.where(cur == m, NEG_INF, cur)
        theta = jnp.max(cur, axis=0, keepdims=True)
        top = a[0] + b[0]
        z = jnp.sum(jnp.where(cand >= theta, jnp.exp(cand - top), 0.0), axis=0, keepdims=True)
        tau = theta - s1
        n = jnp.zeros_like(s1)
        for l in range(PEER_TOPK):
            n = n + jnp.where(b[l] >= tau, 1.0, 0.0)
        rows = slice(h * nk, (h + 1) * nk)
        rows32 = slice(h * nk // 2, (h + 1) * nk // 2)
        rank_ref[0, rows32, :] = pltpu.bitcast(rank2.astype(BF16), jnp.uint32)
        e2_ref[0, rows32, :] = pltpu.bitcast(jnp.exp(s2 - b[0]).astype(BF16), jnp.uint32)
        tb = n.shape[1]
        n_ref[0, :, h, :, :] = n.reshape(nk // 8, 8, tb)
        c1_ref[0, :, h, :, :] = (jnp.exp(s1 - a[0]) / z).reshape(nk // 8, 8, tb)


def _route(hb, wpq, keys, tb):
    t = hb.shape[0]
    nblk = t // tb
    rows = PEER_HEADS * PEER_NKEYS
    spec = lambda r: pl.BlockSpec((1, r, tb), lambda i: (i, 0, 0))
    shape = lambda r, dt: jax.ShapeDtypeStruct((nblk, r, tb), dt)
    key_dims = (PEER_NKEYS // 8, PEER_HEADS, 8, tb)
    key_spec = pl.BlockSpec((1,) + key_dims, lambda i: (i, 0, 0, 0, 0))
    key_shape = jax.ShapeDtypeStruct((nblk,) + key_dims, F32)
    return pl.pallas_call(
        _route_kernel,
        grid=(nblk,),
        in_specs=[pl.BlockSpec((tb, D_MODEL), lambda i: (i, 0)), _full(wpq.shape), _full(keys.shape)],
        out_specs=[spec(rows // 2), spec(rows // 2), key_spec, key_spec],
        out_shape=[shape(rows // 2, jnp.uint32), shape(rows // 2, jnp.uint32), key_shape, key_shape],
        compiler_params=_cparams(("parallel",)),
        name="peer_route",
    )(hb, wpq, keys)


def _peer_kernel(hb_ref, rank_ref, e2_ref, n_ref, c1_ref, u_ref, vt_ref, r2_ref, l2g_ref, l2b_ref,
                 y_ref, acc_ref, a_ref, p_ref, *, ec, tb):
    c = pl.program_id(1)
    nk = PEER_NKEYS

    @pl.when(c == 0)
    def _():
        acc_ref[...] = jnp.zeros_like(acc_ref)

    a_ref[...] = _dot_nt(u_ref[...], hb_ref[...])
    assert ec // nk == 8
    for tt in range(tb // LANES):
        cols = slice(tt * LANES, (tt + 1) * LANES)
        n8 = [n_ref[0, 0, h, :, cols] for h in range(PEER_HEADS)]
        c18 = [c1_ref[0, 0, h, :, cols] for h in range(PEER_HEADS)]
        for ii in range(8):
            w = jnp.zeros((nk, LANES), BF16)
            for h in range(PEER_HEADS):
                rows32 = slice(h * nk // 2, (h + 1) * nk // 2)
                rank = pltpu.bitcast(rank_ref[0, rows32, cols], BF16)
                e2 = pltpu.bitcast(e2_ref[0, rows32, cols], BF16)
                n_row = n8[h][ii:ii + 1, :].astype(BF16)
                c1_row = c18[h][ii:ii + 1, :].astype(BF16)
                w = w + c1_row * jnp.where(rank < n_row, e2, jnp.zeros((), BF16))
            av = a_ref[ii * nk:(ii + 1) * nk, cols]
            gelu = 0.5 * av * (1.0 + lax.erf(av * (2.0 ** -0.5)))
            p_ref[ii * nk:(ii + 1) * nk, cols] = gelu.astype(BF16) * w
    acc_ref[...] += _dot(vt_ref[...], p_ref[...])

    @pl.when(c == pl.num_programs(1) - 1)
    def _():
        ffn = acc_ref[...].T
        y_ref[...] = _layer_norm(r2_ref[...] + ffn, l2g_ref[...], l2b_ref[...])


def _peer(hb, rank2, e2, n, c1, u_b, vt_b, r2, l2g, l2b, tb, ec):
    t = hb.shape[0]
    nblk = t // tb
    rows = PEER_HEADS * PEER_NKEYS
    route_spec = lambda r: pl.BlockSpec((1, r, tb), lambda i, c: (i, 0, 0))
    key_spec = pl.BlockSpec((1, 1, PEER_HEADS, 8, tb), lambda i, c: (i, c, 0, 0, 0))
    return pl.pallas_call(
        functools.partial(_peer_kernel, ec=ec, tb=tb),
        grid=(nblk, PEER_EXPERTS // ec),
        in_specs=[pl.BlockSpec((tb, D_MODEL), lambda i, c: (i, 0)),
                  route_spec(rows // 2), route_spec(rows // 2), key_spec, key_spec,
                  pl.BlockSpec((ec, D_MODEL), lambda i, c: (c, 0)),
                  pl.BlockSpec((D_MODEL, ec), lambda i, c: (0, c)),
                  pl.BlockSpec((tb, D_MODEL), lambda i, c: (i, 0)),
                  _full(l2g.shape), _full(l2b.shape)],
        out_specs=pl.BlockSpec((tb, D_MODEL), lambda i, c: (i, 0)),
        out_shape=jax.ShapeDtypeStruct((t, D_MODEL), F32),
        scratch_shapes=[pltpu.VMEM((D_MODEL, tb), F32), pltpu.VMEM((ec, tb), F32), pltpu.VMEM((ec, tb), BF16)],
        compiler_params=_cparams(("parallel", "arbitrary")),
        name="peer_experts",
    )(hb, rank2, e2, n, c1, u_b, vt_b, r2, l2g, l2b)


def _block_sizes(bsz, seq):
    t = bsz * seq
    return dict(
        tb_in=min(512, t),
        tbg=min(512, seq),
        tm=min(1024, seq), tk=min(1024, seq),
        tb_merge=min(256, t),
        tb_peer=min(512, t),
        ec=8 * PEER_NKEYS,
    )


def _prep_weights(w_in, w_af2, b_af, w_ab2, b_ab, gla_norm_g, w_ao, w_fo, w_mg, b_mg, w_o,
                  ln1_g, ln1_b, w_pq, peer_keys, peer_u, peer_v, w_pg, w_pe, ln2_g, ln2_b):
    r = GLA_GATE_RANK
    zeros = jnp.zeros((r, QK_WIDTH), F32)
    w2 = jnp.concatenate([jnp.concatenate([w_af2, zeros], axis=1),
                          jnp.concatenate([zeros, w_ab2], axis=1)], axis=0)
    return dict(
        wq=w_in[:, :OFF_AF].astype(BF16),
        wg=w_in[:, OFF_AF:OFF_F].astype(BF16),
        wf=w_in[:, OFF_F:].astype(BF16),
        w2=w2.astype(BF16),
        b2=jnp.concatenate([b_af, b_ab])[None, :],
        gn=gla_norm_g[None, :],
        wao=w_ao.astype(BF16), wfo=w_fo.astype(BF16), wmg=w_mg.astype(BF16), bmg=b_mg[None, :],
        wo=w_o.astype(BF16), l1g=ln1_g[None, :], l1b=ln1_b[None, :],
        wpq=w_pq.astype(BF16),
        keys=peer_keys.reshape(PEER_HEADS * 2, PEER_NKEYS, PEER_QDIM // 2).astype(BF16),
        u=peer_u.astype(BF16), vt=peer_v.astype(BF16).T,
        wpg=w_pg.astype(BF16), wpe=w_pe.astype(BF16), l2g=ln2_g[None, :], l2b=ln2_b[None, :],
    )


def _encoder_layer(x, p, w):
    bsz, seq, d = x.shape
    t = bsz * seq
    bs = _block_sizes(bsz, seq)
    x2 = x.reshape(t, d)
    p2 = p.reshape(t, PLE_DIM)
    qkvr, la, f = _inproj(x2, w["wq"], w["wg"], w["wf"], w["w2"], w["b2"], bs["tb_in"])
    o_f, o_b = _gla(qkvr, la, bsz, seq, bs["tbg"])
    fr = _fnet(f, bsz, seq, bs["tm"], bs["tk"])
    hb, r2 = _merge(x2, o_f, o_b, qkvr, fr, p2, w["gn"], w["wao"], w["wfo"], w["wmg"], w["bmg"],
                    w["wo"], w["l1g"], w["l1b"], w["wpg"], w["wpe"], bs["tb_merge"])
    tbp = bs["tb_peer"]
    rank2, e2, n, c1 = _route(hb, w["wpq"], w["keys"], tbp)
    y = _peer(hb, rank2, e2, n, c1, w["u"], w["vt"], r2, w["l2g"], w["l2b"], tbp, bs["ec"])
    return y.reshape(bsz, seq, d)


def kernel(x_prompt, x_sample, p_prompt, p_sample, w_in, w_af2, b_af, w_ab2, b_ab, gla_norm_g, w_ao, w_fo, w_mg, b_mg, w_o, ln1_g, ln1_b, w_pq, peer_keys, peer_u, peer_v, w_pg, w_pe, ln2_g, ln2_b):
    params = (w_in, w_af2, b_af, w_ab2, b_ab, gla_norm_g, w_ao, w_fo, w_mg, b_mg, w_o,
              ln1_g, ln1_b, w_pq, peer_keys, peer_u, peer_v, w_pg, w_pe, ln2_g, ln2_b)
    xs = (x_prompt, x_sample)
    ps = (p_prompt, p_sample)
    for i in range(DEPTH):
        w = _prep_weights(*[a[i] for a in params])
        xs = tuple(_encoder_layer(x, p[i], w) for x, p in zip(xs, ps))
    return xs
```

```python
import functools
import math

import jax
import jax.numpy as jnp
from jax import lax
from jax.experimental import pallas as pl
from jax.experimental.pallas import tpu as pltpu

F32 = jnp.float32
BF16 = jnp.bfloat16

D_MODEL = 1024
DEPTH = 1
GLA_HEADS = 4
GLA_DK = 64
GLA_DV = 128
GLA_GATE_RANK = 16
GLA_GATE_NORMALIZER = 16.0
GLA_CHUNK = 64
FNET_GROUPS = 4
FNET_GROUP_DIM = 128
PEER_HEADS = 8
PEER_NKEYS = 128
PEER_EXPERTS = PEER_NKEYS * PEER_NKEYS
PEER_QDIM = 256
PEER_TOPK = 16
PLE_DIM = 256
LN_EPS = 1e-5
RMS_EPS = 1e-6
DEEPNORM_ALPHA = (2.0 * DEPTH) ** 0.25

QK_WIDTH = GLA_HEADS * GLA_DK
V_WIDTH = GLA_HEADS * GLA_DV
F_WIDTH = FNET_GROUPS * FNET_GROUP_DIM
OFF_AF = 2 * QK_WIDTH + 2 * V_WIDTH
OFF_F = OFF_AF + 2 * GLA_GATE_RANK

VMEM_LIMIT_BYTES = 56 * 1024 * 1024
LANES = 128

NEG_INF = float("-inf")


def _cparams(sem, flags=None):
    return pltpu.CompilerParams(dimension_semantics=sem, vmem_limit_bytes=VMEM_LIMIT_BYTES, flags=flags)


def _full(shape):
    nd = len(shape)
    return pl.BlockSpec(shape, lambda *_: (0,) * nd)


def _dot(a, b):
    return jnp.dot(a, b, preferred_element_type=F32)


def _dot_nt(a, b):
    return lax.dot_general(a, b, (((1,), (1,)), ((), ())), preferred_element_type=F32)


def _dot_tn(a, b):
    return lax.dot_general(a, b, (((0,), (0,)), ((), ())), preferred_element_type=F32)


def _layer_norm(x, g, b):
    mu = jnp.mean(x, axis=-1, keepdims=True)
    xc = x - mu
    var = jnp.mean(xc * xc, axis=-1, keepdims=True)
    return xc * lax.rsqrt(var + LN_EPS) * g + b


def _inproj_kernel(x_ref, wq_ref, wg_ref, wf_ref, w2_ref, b2_ref, qkvr_ref, la_ref, f_ref):
    xb = x_ref[...].astype(BF16)
    qkvr_ref[...] = _dot(xb, wq_ref[...])
    g = _dot(xb, wg_ref[...])
    z = _dot(g.astype(BF16), w2_ref[...]) + b2_ref[...]
    log_sig = jnp.minimum(z, 0.0) - jnp.log1p(jnp.exp(-jnp.abs(z)))
    la_ref[...] = log_sig * (1.0 / GLA_GATE_NORMALIZER)
    f_ref[...] = _dot(xb, wf_ref[...]).astype(BF16)


def _inproj(x2, wq, wg, wf, w2, b2, tb):
    t = x2.shape[0]
    nq = wq.shape[1]
    return pl.pallas_call(
        _inproj_kernel,
        grid=(t // tb,),
        in_specs=[
            pl.BlockSpec((tb, D_MODEL), lambda i: (i, 0)),
            _full(wq.shape), _full(wg.shape), _full(wf.shape), _full(w2.shape), _full(b2.shape),
        ],
        out_specs=[
            pl.BlockSpec((tb, nq), lambda i: (i, 0)),
            pl.BlockSpec((tb, 2 * QK_WIDTH), lambda i: (i, 0)),
            pl.BlockSpec((tb, F_WIDTH), lambda i: (i, 0)),
        ],
        out_shape=[
            jax.ShapeDtypeStruct((t, nq), F32),
            jax.ShapeDtypeStruct((t, 2 * QK_WIDTH), F32),
            jax.ShapeDtypeStruct((t, F_WIDTH), BF16),
        ],
        compiler_params=_cparams(("parallel",)),
        name="inproj",
    )(x2, wq, wg, wf, w2, b2)


def _gla_chunk(q, k, v, la, tri, smask, hmask, bmask, st_ref, mid, last):
    c = GLA_CHUNK
    la_hi = la.astype(BF16)
    la_lo = (la - la_hi.astype(F32)).astype(BF16)
    bc = _dot(tri, la_hi) + _dot(tri, la_lo)
    b_mid = bc[mid:mid + 1, :]
    b_last = bc[last:last + 1, :]
    qs = q * (GLA_DK ** -0.5)
    vb = v.astype(BF16)
    qd = qs * jnp.exp(bc - b_mid)
    kd = (k * jnp.exp(b_mid - bc)).astype(BF16)
    qstack = jnp.concatenate(
        [jnp.where(hmask[h:h + 1, :] > 0.0, qd, 0.0) for h in range(GLA_HEADS)], axis=0).astype(BF16)
    scores = _dot_nt(qstack, kd)
    scores = jnp.where(smask > 0.0, scores, 0.0).astype(BF16)
    oi_all = _dot(scores, vb)
    o_intra = jnp.concatenate(
        [oi_all[h * c:(h + 1) * c, h * GLA_DV:(h + 1) * GLA_DV] for h in range(GLA_HEADS)], axis=1)
    q_in = (qs * jnp.exp(bc)).astype(BF16)
    st = st_ref[...]
    o_inter = _dot_nt(q_in, st.astype(BF16))
    k_st = (k * jnp.exp(b_last - bc)).astype(BF16)
    upd = _dot_tn(vb, k_st)
    st_ref[...] = st * jnp.exp(b_last) + jnp.where(bmask > 0.0, upd, 0.0)
    return o_intra + o_inter


def _gla_kernel(qf_ref, kf_ref, vf_ref, laf_ref, qb_ref, kb_ref, vb_ref, lab_ref,
                trif_ref, trib_ref, smf_ref, smb_ref, hm_ref, bm_ref,
                of_ref, ob_ref, stf_ref, stb_ref, *, nchunk):
    @pl.when(pl.program_id(1) == 0)
    def _():
        stf_ref[...] = jnp.zeros_like(stf_ref)
        stb_ref[...] = jnp.zeros_like(stb_ref)

    c = GLA_CHUNK
    trif = trif_ref[...]
    trib = trib_ref[...]
    smf = smf_ref[...]
    smb = smb_ref[...]
    hm = hm_ref[...]
    bm = bm_ref[...]

    def body(j, carry):
        rf = pl.ds(pl.multiple_of(j * c, c), c)
        of_ref[rf, :] = _gla_chunk(qf_ref[rf, :], kf_ref[rf, :], vf_ref[rf, :], laf_ref[rf, :],
                                   trif, smf, hm, bm, stf_ref, c // 2 - 1, c - 1)
        rb = pl.ds(pl.multiple_of((nchunk - 1 - j) * c, c), c)
        ob_ref[rb, :] = _gla_chunk(qb_ref[rb, :], kb_ref[rb, :], vb_ref[rb, :], lab_ref[rb, :],
                                   trib, smb, hm, bm, stb_ref, c // 2, 0)
        return carry

    lax.fori_loop(0, nchunk, body, 0)


def _gla_consts():
    c = GLA_CHUNK
    r = jnp.arange(c)
    trif = (r[:, None] >= r[None, :]).astype(BF16)
    trib = (r[None, :] >= r[:, None]).astype(BF16)
    rr = jnp.arange(GLA_HEADS * c) % c
    smf = (rr[:, None] >= r[None, :]).astype(F32)
    smb = (r[None, :] > rr[:, None]).astype(F32)
    hm = (jnp.arange(QK_WIDTH)[None, :] // GLA_DK == jnp.arange(GLA_HEADS)[:, None]).astype(F32)
    hm = jnp.concatenate([hm, jnp.zeros((8 - GLA_HEADS, QK_WIDTH), F32)], axis=0)
    bm = (jnp.arange(V_WIDTH)[:, None] // GLA_DV == jnp.arange(QK_WIDTH)[None, :] // GLA_DK).astype(F32)
    return trif, trib, smf, smb, hm, bm


def _gla(qkvr, la, bsz, seq, tbg):
    t = bsz * seq
    nblk = seq // tbg
    nchunk = tbg // GLA_CHUNK
    consts = _gla_consts()
    fwd = lambda col: (lambda b, i: (b * nblk + i, col))
    bwd = lambda col: (lambda b, i: (b * nblk + nblk - 1 - i, col))
    in_specs = [
        pl.BlockSpec((tbg, QK_WIDTH), fwd(0)),
        pl.BlockSpec((tbg, QK_WIDTH), fwd(1)),
        pl.BlockSpec((tbg, V_WIDTH), fwd(1)),
        pl.BlockSpec((tbg, QK_WIDTH), fwd(0)),
        pl.BlockSpec((tbg, QK_WIDTH), bwd(0)),
        pl.BlockSpec((tbg, QK_WIDTH), bwd(1)),
        pl.BlockSpec((tbg, V_WIDTH), bwd(1)),
        pl.BlockSpec((tbg, QK_WIDTH), bwd(1)),
    ] + [_full(a.shape) for a in consts]
    return pl.pallas_call(
        functools.partial(_gla_kernel, nchunk=nchunk),
        grid=(bsz, nblk),
        in_specs=in_specs,
        out_specs=[pl.BlockSpec((tbg, V_WIDTH), fwd(0)), pl.BlockSpec((tbg, V_WIDTH), bwd(0))],
        out_shape=[jax.ShapeDtypeStruct((t, V_WIDTH), F32), jax.ShapeDtypeStruct((t, V_WIDTH), F32)],
        scratch_shapes=[pltpu.VMEM((V_WIDTH, QK_WIDTH), F32), pltpu.VMEM((V_WIDTH, QK_WIDTH), F32)],
        compiler_params=_cparams(("parallel", "arbitrary")),
        name="gla",
    )(qkvr, qkvr, qkvr, la, qkvr, qkvr, qkvr, la, *consts)


def _dft_kernel(c_ref, s_ref, f_ref, c128_ref, s128_ref, o_ref, p_acc, q_acc):
    k = pl.program_id(2)

    @pl.when(k == 0)
    def _():
        p_acc[...] = jnp.zeros_like(p_acc)
        q_acc[...] = jnp.zeros_like(q_acc)

    fb = f_ref[...]
    p_acc[...] += _dot(c_ref[...], fb)
    q_acc[...] += _dot(s_ref[...], fb)

    @pl.when(k == pl.num_programs(2) - 1)
    def _():
        c128 = c128_ref[...]
        s128 = s128_ref[...]
        for g in range(FNET_GROUPS):
            sl = slice(g * FNET_GROUP_DIM, (g + 1) * FNET_GROUP_DIM)
            pg = p_acc[:, sl].astype(BF16)
            qg = q_acc[:, sl].astype(BF16)
            o_ref[:, sl] = (_dot(pg, c128) - _dot(qg, s128)).astype(BF16)


def _dft_tables(n, split):
    k = jnp.arange(n, dtype=jnp.int32)
    m_hi = jnp.arange(n // split, dtype=jnp.int32) * split
    m_lo = jnp.arange(split, dtype=jnp.int32)
    w = 2.0 * math.pi / n
    ang_hi = ((m_hi[:, None] * k[None, :]) % n).astype(F32) * w
    ang_lo = ((m_lo[:, None] * k[None, :]) % n).astype(F32) * w
    ch, sh = jnp.cos(ang_hi)[:, None, :], jnp.sin(ang_hi)[:, None, :]
    cl, sl = jnp.cos(ang_lo)[None, :, :], jnp.sin(ang_lo)[None, :, :]
    c = (ch * cl - sh * sl).reshape(n, n)
    s = (sh * cl + ch * sl).reshape(n, n)
    return c.astype(BF16), s.astype(BF16)


def _small_dft_tables(n, scale):
    k = jnp.arange(n, dtype=jnp.int32)
    ang = ((k[:, None] * k[None, :]) % n).astype(F32) * (2.0 * math.pi / n)
    return (jnp.cos(ang) * scale).astype(BF16), (jnp.sin(ang) * scale).astype(BF16)


def _fnet(f, bsz, seq, tm, tk):
    t = bsz * seq
    c_l, s_l = _dft_tables(seq, 64)
    c128, s128 = _small_dft_tables(FNET_GROUP_DIM, (seq * FNET_GROUP_DIM) ** -0.5)
    nm, nk = seq // tm, seq // tk
    return pl.pallas_call(
        _dft_kernel,
        grid=(bsz, nm, nk),
        in_specs=[
            pl.BlockSpec((tm, tk), lambda b, m, k: (m, k)),
            pl.BlockSpec((tm, tk), lambda b, m, k: (m, k)),
            pl.BlockSpec((tk, F_WIDTH), lambda b, m, k: (b * nk + k, 0)),
            _full(c128.shape), _full(s128.shape),
        ],
        out_specs=pl.BlockSpec((tm, F_WIDTH), lambda b, m, k: (b * nm + m, 0)),
        out_shape=jax.ShapeDtypeStruct((t, F_WIDTH), BF16),
        scratch_shapes=[pltpu.VMEM((tm, F_WIDTH), F32), pltpu.VMEM((tm, F_WIDTH), F32)],
        compiler_params=_cparams(("parallel", "parallel", "arbitrary")),
        name="fnet_dft",
    )(c_l, s_l, f, c128, s128)


def _merge_kernel(x_ref, of_ref, ob_ref, r_ref, fr_ref, p_ref,
                  gn_ref, wao_ref, wfo_ref, wmg_ref, bmg_ref, wo_ref, l1g_ref, l1b_ref,
                  wpg_ref, wpe_ref, hb_ref, r2_ref):
    x = x_ref[...]
    xb = x.astype(BF16)
    o = of_ref[...] + ob_ref[...]
    parts = []
    for h in range(GLA_HEADS):
        oh = o[:, h * GLA_DV:(h + 1) * GLA_DV]
        ms = jnp.mean(oh * oh, axis=-1, keepdims=True)
        parts.append(oh * lax.rsqrt(ms + RMS_EPS) * gn_ref[...])
    on = jnp.concatenate(parts, axis=1)
    r = r_ref[...]
    og = (on * (r * jax.nn.sigmoid(r))).astype(BF16)
    branch_gla = _dot(og, wao_ref[...])
    branch_fnet = _dot(fr_ref[...], wfo_ref[...])
    gates = jax.nn.sigmoid(_dot(xb, wmg_ref[...]) + bmg_ref[...])
    merged = gates[:, :D_MODEL] * branch_gla + gates[:, D_MODEL:] * branch_fnet
    mix = _dot(merged.astype(BF16), wo_ref[...])
    h1 = _layer_norm(DEEPNORM_ALPHA * x + mix, l1g_ref[...], l1b_ref[...])
    h1b = h1.astype(BF16)
    hb_ref[...] = h1b
    ple = jax.nn.sigmoid(_dot(h1b, wpg_ref[...])) * _dot(p_ref[...].astype(BF16), wpe_ref[...])
    r2_ref[...] = DEEPNORM_ALPHA * h1 + ple


def _merge(x2, o_f, o_b, qkvr, fr, p2, gn, wao, wfo, wmg, bmg, wo, l1g, l1b, wpg, wpe, tb):
    t = x2.shape[0]
    row = lambda w: pl.BlockSpec((tb, w), lambda i: (i, 0))
    weights = (gn, wao, wfo, wmg, bmg, wo, l1g, l1b, wpg, wpe)
    return pl.pallas_call(
        _merge_kernel,
        grid=(t // tb,),
        in_specs=[row(D_MODEL), row(V_WIDTH), row(V_WIDTH),
                  pl.BlockSpec((tb, V_WIDTH), lambda i: (i, 2)),
                  row(F_WIDTH), row(PLE_DIM)] + [_full(w.shape) for w in weights],
        out_specs=[row(D_MODEL), row(D_MODEL)],
        out_shape=[jax.ShapeDtypeStruct((t, D_MODEL), BF16), jax.ShapeDtypeStruct((t, D_MODEL), F32)],
        compiler_params=_cparams(("parallel",)),
        name="merge_ln1",
    )(x2, o_f, o_b, qkvr, fr, p2, *weights)


def _sort16_pairs():
    def merge(lo, hi, r):
        step = r * 2
        if step < hi - lo:
            yield from merge(lo, hi, step)
            yield from merge(lo + r, hi, step)
            yield from ((i, i + r) for i in range(lo + r, hi - r, step))
        else:
            yield (lo, lo + r)

    def sort(lo, hi):
        if hi - lo >= 1:
            mid = lo + (hi - lo) // 2
            yield from sort(lo, mid)
            yield from sort(mid + 1, hi)
            yield from merge(lo, hi, 1)

    return tuple(sort(0, PEER_TOPK - 1))


def _top16_desc(s):
    assert s.shape[0] == 8 * PEER_TOPK
    v = [s[8 * k:8 * (k + 1), :] for k in range(PEER_TOPK)]
    for i, j in _sort16_pairs():
        v[i], v[j] = jnp.maximum(v[i], v[j]), jnp.minimum(v[i], v[j])
    vals = []
    for r in range(PEER_TOPK):
        m = jnp.max(v[0], axis=0, keepdims=True)
        vals.append(m)
        if r == PEER_TOPK - 1:
            break
        hit = v[0] == m
        last = PEER_TOPK - 1 - r
        for k in range(last):
            v[k] = jnp.where(hit, v[k + 1], v[k])
        v[last] = jnp.where(hit, NEG_INF, v[last])
    return vals


def _count(b, x, strict):
    assert len(b) == 16
    ge = (lambda u: u > x) if strict else (lambda u: u >= x)
    c8 = ge(b[7])
    c4 = ge(jnp.where(c8, b[11], b[3]))
    c2 = ge(jnp.where(c8, jnp.where(c4, b[13], b[9]), jnp.where(c4, b[5], b[1])))
    hi = jnp.where(c4, jnp.where(c2, b[14], b[12]), jnp.where(c2, b[10], b[8]))
    lo = jnp.where(c4, jnp.where(c2, b[6], b[4]), jnp.where(c2, b[2], b[0]))
    c1 = ge(jnp.where(c8, hi, lo))
    pos = (jnp.where(c8, 8.0, 0.0) + jnp.where(c4, 4.0, 0.0)) + (jnp.where(c2, 2.0, 0.0) + jnp.where(c1, 1.0, 0.0))
    return jnp.where(ge(b[15]), 16.0, pos)


def _route_kernel(hb_ref, wpq_ref, keys_ref, rank_ref, e2_ref, n_ref, c1_ref):
    qh = _dot(hb_ref[...], wpq_ref[...]).astype(BF16)
    nk = PEER_NKEYS
    for h in range(PEER_HEADS):
        q1 = qh[:, (2 * h) * nk:(2 * h + 1) * nk]
        q2 = qh[:, (2 * h + 1) * nk:(2 * h + 2) * nk]
        s1 = _dot_nt(keys_ref[2 * h], q1)
        s2 = _dot_nt(keys_ref[2 * h + 1], q2)
        a = _top16_desc(s1)
        b = _top16_desc(s2)
        rank2 = _count(b, s2, strict=True)
        cands = []
        for k in range(PEER_TOPK):
            for l in range(PEER_TOPK // (k + 1)):
                cands.append(a[k] + b[l])
        pad = (-len(cands)) % 8
        cand = jnp.concatenate(cands + [jnp.full_like(a[0], NEG_INF)] * pad, axis=0)
        cur = cand
        for _ in range(PEER_TOPK - 1):
            m = jnp.max(cur, axis=0, keepdims=True)
            cur = jnp.where(cur == m, NEG_INF, cur)
        theta = jnp.max(cur, axis=0, keepdims=True)
        top = a[0] + b[0]
        z = jnp.sum(jnp.where(cand >= theta, jnp.exp(cand - top), 0.0), axis=0, keepdims=True)
        n = _count(b, theta - s1, strict=False)
        rows32 = slice(h * nk // 2, (h + 1) * nk // 2)
        rank_ref[0, rows32, :] = pltpu.bitcast(rank2.astype(BF16), jnp.uint32)
        e2_ref[0, rows32, :] = pltpu.bitcast(jnp.exp(s2 - b[0]).astype(BF16), jnp.uint32)
        tb = n.shape[1]
        n_ref[0, :, h, :, :] = n.reshape(nk // 8, 8, tb)
        c1_ref[0, :, h, :, :] = (jnp.exp(s1 - a[0]) / z).reshape(nk // 8, 8, tb)


def _route(hb, wpq, keys, tb):
    t = hb.shape[0]
    nblk = t // tb
    rows = PEER_HEADS * PEER_NKEYS
    spec = lambda r: pl.BlockSpec((1, r, tb), lambda i: (i, 0, 0))
    shape = lambda r, dt: jax.ShapeDtypeStruct((nblk, r, tb), dt)
    key_dims = (PEER_NKEYS // 8, PEER_HEADS, 8, tb)
    key_spec = pl.BlockSpec((1,) + key_dims, lambda i: (i, 0, 0, 0, 0))
    key_shape = jax.ShapeDtypeStruct((nblk,) + key_dims, F32)
    return pl.pallas_call(
        _route_kernel,
        grid=(nblk,),
        in_specs=[pl.BlockSpec((tb, D_MODEL), lambda i: (i, 0)), _full(wpq.shape), _full(keys.shape)],
        out_specs=[spec(rows // 2), spec(rows // 2), key_spec, key_spec],
        out_shape=[shape(rows // 2, jnp.uint32), shape(rows // 2, jnp.uint32), key_shape, key_shape],
        compiler_params=_cparams(("parallel",)),
        name="peer_route",
    )(hb, wpq, keys)


def _peer_kernel(hb_ref, rank_ref, e2_ref, n_ref, c1_ref, u_ref, vt_ref, r2_ref, l2g_ref, l2b_ref,
                 y_ref, acc_ref, a_ref, p_ref, *, ec, tb):
    c = pl.program_id(1)
    nk = PEER_NKEYS

    @pl.when(c == 0)
    def _():
        acc_ref[...] = jnp.zeros_like(acc_ref)

    a_ref[...] = _dot_nt(u_ref[...], hb_ref[...])
    for tt in range(tb // LANES):
        cols = slice(tt * LANES, (tt + 1) * LANES)
        n8 = [[n_ref[0, g, h, :, cols] for h in range(PEER_HEADS)] for g in range(ec // (8 * nk))]
        c18 = [[c1_ref[0, g, h, :, cols] for h in range(PEER_HEADS)] for g in range(ec // (8 * nk))]
        for ii in range(ec // nk):
            w = jnp.zeros((nk, LANES), BF16)
            for h in range(PEER_HEADS):
                rows32 = slice(h * nk // 2, (h + 1) * nk // 2)
                rank = pltpu.bitcast(rank_ref[0, rows32, cols], BF16)
                e2 = pltpu.bitcast(e2_ref[0, rows32, cols], BF16)
                n_row = n8[ii // 8][h][ii % 8:ii % 8 + 1, :].astype(BF16)
                c1_row = c18[ii // 8][h][ii % 8:ii % 8 + 1, :].astype(BF16)
                w = w + c1_row * jnp.where(rank < n_row, e2, jnp.zeros((), BF16))
            av = a_ref[ii * nk:(ii + 1) * nk, cols]
            gelu = 0.5 * av * (1.0 + lax.erf(av * (2.0 ** -0.5)))
            p_ref[ii * nk:(ii + 1) * nk, cols] = gelu.astype(BF16) * w
    acc_ref[...] += _dot(vt_ref[...], p_ref[...])

    @pl.when(c == pl.num_programs(1) - 1)
    def _():
        ffn = acc_ref[...].T
        y_ref[...] = _layer_norm(r2_ref[...] + ffn, l2g_ref[...], l2b_ref[...])


def _peer(hb, rank2, e2, n, c1, u_b, vt_b, r2, l2g, l2b, tb, ec):
    t = hb.shape[0]
    nblk = t // tb
    rows = PEER_HEADS * PEER_NKEYS
    route_spec = lambda r: pl.BlockSpec((1, r, tb), lambda i, c: (i, 0, 0))
    key_spec = pl.BlockSpec((1, ec // (8 * PEER_NKEYS), PEER_HEADS, 8, tb), lambda i, c: (i, c, 0, 0, 0))
    return pl.pallas_call(
        functools.partial(_peer_kernel, ec=ec, tb=tb),
        grid=(nblk, PEER_EXPERTS // ec),
        in_specs=[pl.BlockSpec((tb, D_MODEL), lambda i, c: (i, 0)),
                  route_spec(rows // 2), route_spec(rows // 2), key_spec, key_spec,
                  pl.BlockSpec((ec, D_MODEL), lambda i, c: (c, 0)),
                  pl.BlockSpec((D_MODEL, ec), lambda i, c: (0, c)),
                  pl.BlockSpec((tb, D_MODEL), lambda i, c: (i, 0)),
                  _full(l2g.shape), _full(l2b.shape)],
        out_specs=pl.BlockSpec((tb, D_MODEL), lambda i, c: (i, 0)),
        out_shape=jax.ShapeDtypeStruct((t, D_MODEL), F32),
        scratch_shapes=[pltpu.VMEM((D_MODEL, tb), F32), pltpu.VMEM((ec, tb), F32), pltpu.VMEM((ec, tb), BF16)],
        compiler_params=_cparams(("parallel", "arbitrary")),
        name="peer_experts",
    )(hb, rank2, e2, n, c1, u_b, vt_b, r2, l2g, l2b)


def _block_sizes(bsz, seq):
    t = bsz * seq
    return dict(
        tb_in=min(512, t),
        tbg=min(512, seq),
        tm=min(1024, seq), tk=min(1024, seq),
        tb_merge=min(256, t),
        tb_peer=min(512, t),
        ec=16 * PEER_NKEYS,
    )


def _prep_weights(w_in, w_af2, b_af, w_ab2, b_ab, gla_norm_g, w_ao, w_fo, w_mg, b_mg, w_o,
                  ln1_g, ln1_b, w_pq, peer_keys, peer_u, peer_v, w_pg, w_pe, ln2_g, ln2_b):
    r = GLA_GATE_RANK
    zeros = jnp.zeros((r, QK_WIDTH), F32)
    w2 = jnp.concatenate([jnp.concatenate([w_af2, zeros], axis=1),
                          jnp.concatenate([zeros, w_ab2], axis=1)], axis=0)
    return dict(
        wq=w_in[:, :OFF_AF].astype(BF16),
        wg=w_in[:, OFF_AF:OFF_F].astype(BF16),
        wf=w_in[:, OFF_F:].astype(BF16),
        w2=w2.astype(BF16),
        b2=jnp.concatenate([b_af, b_ab])[None, :],
        gn=gla_norm_g[None, :],
        wao=w_ao.astype(BF16), wfo=w_fo.astype(BF16), wmg=w_mg.astype(BF16), bmg=b_mg[None, :],
        wo=w_o.astype(BF16), l1g=ln1_g[None, :], l1b=ln1_b[None, :],
        wpq=w_pq.astype(BF16),
        keys=peer_keys.reshape(PEER_HEADS * 2, PEER_NKEYS, PEER_QDIM // 2).astype(BF16),
        u=peer_u.astype(BF16), vt=peer_v.astype(BF16).T,
        wpg=w_pg.astype(BF16), wpe=w_pe.astype(BF16), l2g=ln2_g[None, :], l2b=ln2_b[None, :],
    )


def _encoder_layer(x, p, w):
    bsz, seq, d = x.shape
    t = bsz * seq
    bs = _block_sizes(bsz, seq)
    x2 = x.reshape(t, d)
    p2 = p.reshape(t, PLE_DIM)
    qkvr, la, f = _inproj(x2, w["wq"], w["wg"], w["wf"], w["w2"], w["b2"], bs["tb_in"])
    o_f, o_b = _gla(qkvr, la, bsz, seq, bs["tbg"])
    fr = _fnet(f, bsz, seq, bs["tm"], bs["tk"])
    hb, r2 = _merge(x2, o_f, o_b, qkvr, fr, p2, w["gn"], w["wao"], w["wfo"], w["wmg"], w["bmg"],
                    w["wo"], w["l1g"], w["l1b"], w["wpg"], w["wpe"], bs["tb_merge"])
    tbp = bs["tb_peer"]
    rank2, e2, n, c1 = _route(hb, w["wpq"], w["keys"], tbp)
    y = _peer(hb, rank2, e2, n, c1, w["u"], w["vt"], r2, w["l2g"], w["l2b"], tbp, bs["ec"])
    return y.reshape(bsz, seq, d)


def kernel(x_prompt, x_sample, p_prompt, p_sample, w_in, w_af2, b_af, w_ab2, b_ab, gla_norm_g, w_ao, w_fo, w_mg, b_mg, w_o, ln1_g, ln1_b, w_pq, peer_keys, peer_u, peer_v, w_pg, w_pe, ln2_g, ln2_b):
    params = (w_in, w_af2, b_af, w_ab2, b_ab, gla_norm_g, w_ao, w_fo, w_mg, b_mg, w_o,
              ln1_g, ln1_b, w_pq, peer_keys, peer_u, peer_v, w_pg, w_pe, ln2_g, ln2_b)
    xs = (x_prompt, x_sample)
    ps = (p_prompt, p_sample)
    for i in range(DEPTH):
        w = _prep_weights(*[a[i] for a in params])
        xs = tuple(_encoder_layer(x, p[i], w) for x, p in zip(xs, ps))
    return xs
```

```python
import functools
import math

import jax
import jax.numpy as jnp
from jax import lax
from jax.experimental import pallas as pl
from jax.experimental.pallas import tpu as pltpu

F32 = jnp.float32
BF16 = jnp.bfloat16

D_MODEL = 1024
DEPTH = 1
GLA_HEADS = 4
GLA_DK = 64
GLA_DV = 128
GLA_GATE_RANK = 16
GLA_GATE_NORMALIZER = 16.0
GLA_CHUNK = 64
FNET_GROUPS = 4
FNET_GROUP_DIM = 128
PEER_HEADS = 8
PEER_NKEYS = 128
PEER_EXPERTS = PEER_NKEYS * PEER_NKEYS
PEER_QDIM = 256
PEER_TOPK = 16
PLE_DIM = 256
LN_EPS = 1e-5
RMS_EPS = 1e-6
DEEPNORM_ALPHA = (2.0 * DEPTH) ** 0.25

QK_WIDTH = GLA_HEADS * GLA_DK
V_WIDTH = GLA_HEADS * GLA_DV
F_WIDTH = FNET_GROUPS * FNET_GROUP_DIM
OFF_AF = 2 * QK_WIDTH + 2 * V_WIDTH
OFF_F = OFF_AF + 2 * GLA_GATE_RANK

VMEM_LIMIT_BYTES = 56 * 1024 * 1024
LANES = 128

NEG_INF = float("-inf")


def _cparams(sem, flags=None):
    return pltpu.CompilerParams(dimension_semantics=sem, vmem_limit_bytes=VMEM_LIMIT_BYTES, flags=flags)


def _full(shape):
    nd = len(shape)
    return pl.BlockSpec(shape, lambda *_: (0,) * nd)


def _dot(a, b):
    return jnp.dot(a, b, preferred_element_type=F32)


def _dot_nt(a, b):
    return lax.dot_general(a, b, (((1,), (1,)), ((), ())), preferred_element_type=F32)


def _dot_tn(a, b):
    return lax.dot_general(a, b, (((0,), (0,)), ((), ())), preferred_element_type=F32)


def _layer_norm(x, g, b):
    mu = jnp.mean(x, axis=-1, keepdims=True)
    xc = x - mu
    var = jnp.mean(xc * xc, axis=-1, keepdims=True)
    return xc * lax.rsqrt(var + LN_EPS) * g + b


def _inproj_kernel(x_ref, wq_ref, wg_ref, wf_ref, w2_ref, b2_ref, qkvr_ref, la_ref, f_ref):
    xb = x_ref[...].astype(BF16)
    qkvr_ref[...] = _dot(xb, wq_ref[...])
    g = _dot(xb, wg_ref[...])
    z = _dot(g.astype(BF16), w2_ref[...]) + b2_ref[...]
    log_sig = jnp.minimum(z, 0.0) - jnp.log1p(jnp.exp(-jnp.abs(z)))
    la_ref[...] = log_sig * (1.0 / GLA_GATE_NORMALIZER)
    f_ref[...] = _dot(xb, wf_ref[...]).astype(BF16)


def _inproj(x2, wq, wg, wf, w2, b2, tb):
    t = x2.shape[0]
    nq = wq.shape[1]
    return pl.pallas_call(
        _inproj_kernel,
        grid=(t // tb,),
        in_specs=[
            pl.BlockSpec((tb, D_MODEL), lambda i: (i, 0)),
            _full(wq.shape), _full(wg.shape), _full(wf.shape), _full(w2.shape), _full(b2.shape),
        ],
        out_specs=[
            pl.BlockSpec((tb, nq), lambda i: (i, 0)),
            pl.BlockSpec((tb, 2 * QK_WIDTH), lambda i: (i, 0)),
            pl.BlockSpec((tb, F_WIDTH), lambda i: (i, 0)),
        ],
        out_shape=[
            jax.ShapeDtypeStruct((t, nq), F32),
            jax.ShapeDtypeStruct((t, 2 * QK_WIDTH), F32),
            jax.ShapeDtypeStruct((t, F_WIDTH), BF16),
        ],
        compiler_params=_cparams(("parallel",)),
        name="inproj",
    )(x2, wq, wg, wf, w2, b2)


def _gla_chunk(q, k, v, la, tri, smask, hmask, bmask, st_ref, mid, last):
    c = GLA_CHUNK
    la_hi = la.astype(BF16)
    la_lo = (la - la_hi.astype(F32)).astype(BF16)
    bc = _dot(tri, la_hi) + _dot(tri, la_lo)
    b_mid = bc[mid:mid + 1, :]
    b_last = bc[last:last + 1, :]
    qs = q * (GLA_DK ** -0.5)
    vb = v.astype(BF16)
    qd = qs * jnp.exp(bc - b_mid)
    kd = (k * jnp.exp(b_mid - bc)).astype(BF16)
    qstack = jnp.concatenate(
        [jnp.where(hmask[h:h + 1, :] > 0.0, qd, 0.0) for h in range(GLA_HEADS)], axis=0).astype(BF16)
    scores = _dot_nt(qstack, kd)
    scores = jnp.where(smask > 0.0, scores, 0.0).astype(BF16)
    oi_all = _dot(scores, vb)
    o_intra = jnp.concatenate(
        [oi_all[h * c:(h + 1) * c, h * GLA_DV:(h + 1) * GLA_DV] for h in range(GLA_HEADS)], axis=1)
    q_in = (qs * jnp.exp(bc)).astype(BF16)
    st = st_ref[...]
    o_inter = _dot_nt(q_in, st.astype(BF16))
    k_st = (k * jnp.exp(b_last - bc)).astype(BF16)
    upd = _dot_tn(vb, k_st)
    st_ref[...] = st * jnp.exp(b_last) + jnp.where(bmask > 0.0, upd, 0.0)
    return o_intra + o_inter


def _gla_kernel(qf_ref, kf_ref, vf_ref, laf_ref, qb_ref, kb_ref, vb_ref, lab_ref,
                trif_ref, trib_ref, smf_ref, smb_ref, hm_ref, bm_ref,
                of_ref, ob_ref, stf_ref, stb_ref, *, nchunk):
    @pl.when(pl.program_id(1) == 0)
    def _():
        stf_ref[...] = jnp.zeros_like(stf_ref)
        stb_ref[...] = jnp.zeros_like(stb_ref)

    c = GLA_CHUNK
    trif = trif_ref[...]
    trib = trib_ref[...]
    smf = smf_ref[...]
    smb = smb_ref[...]
    hm = hm_ref[...]
    bm = bm_ref[...]

    def body(j, carry):
        rf = pl.ds(pl.multiple_of(j * c, c), c)
        of_ref[rf, :] = _gla_chunk(qf_ref[rf, :], kf_ref[rf, :], vf_ref[rf, :], laf_ref[rf, :],
                                   trif, smf, hm, bm, stf_ref, c // 2 - 1, c - 1)
        rb = pl.ds(pl.multiple_of((nchunk - 1 - j) * c, c), c)
        ob_ref[rb, :] = _gla_chunk(qb_ref[rb, :], kb_ref[rb, :], vb_ref[rb, :], lab_ref[rb, :],
                                   trib, smb, hm, bm, stb_ref, c // 2, 0)
        return carry

    lax.fori_loop(0, nchunk, body, 0, unroll=4)


def _gla_consts():
    c = GLA_CHUNK
    r = jnp.arange(c)
    trif = (r[:, None] >= r[None, :]).astype(BF16)
    trib = (r[None, :] >= r[:, None]).astype(BF16)
    rr = jnp.arange(GLA_HEADS * c) % c
    smf = (rr[:, None] >= r[None, :]).astype(F32)
    smb = (r[None, :] > rr[:, None]).astype(F32)
    hm = (jnp.arange(QK_WIDTH)[None, :] // GLA_DK == jnp.arange(GLA_HEADS)[:, None]).astype(F32)
    hm = jnp.concatenate([hm, jnp.zeros((8 - GLA_HEADS, QK_WIDTH), F32)], axis=0)
    bm = (jnp.arange(V_WIDTH)[:, None] // GLA_DV == jnp.arange(QK_WIDTH)[None, :] // GLA_DK).astype(F32)
    return trif, trib, smf, smb, hm, bm


def _gla(qkvr, la, bsz, seq, tbg):
    t = bsz * seq
    nblk = seq // tbg
    nchunk = tbg // GLA_CHUNK
    consts = _gla_consts()
    fwd = lambda col: (lambda b, i: (b * nblk + i, col))
    bwd = lambda col: (lambda b, i: (b * nblk + nblk - 1 - i, col))
    in_specs = [
        pl.BlockSpec((tbg, QK_WIDTH), fwd(0)),
        pl.BlockSpec((tbg, QK_WIDTH), fwd(1)),
        pl.BlockSpec((tbg, V_WIDTH), fwd(1)),
        pl.BlockSpec((tbg, QK_WIDTH), fwd(0)),
        pl.BlockSpec((tbg, QK_WIDTH), bwd(0)),
        pl.BlockSpec((tbg, QK_WIDTH), bwd(1)),
        pl.BlockSpec((tbg, V_WIDTH), bwd(1)),
        pl.BlockSpec((tbg, QK_WIDTH), bwd(1)),
    ] + [_full(a.shape) for a in consts]
    return pl.pallas_call(
        functools.partial(_gla_kernel, nchunk=nchunk),
        grid=(bsz, nblk),
        in_specs=in_specs,
        out_specs=[pl.BlockSpec((tbg, V_WIDTH), fwd(0)), pl.BlockSpec((tbg, V_WIDTH), bwd(0))],
        out_shape=[jax.ShapeDtypeStruct((t, V_WIDTH), F32), jax.ShapeDtypeStruct((t, V_WIDTH), F32)],
        scratch_shapes=[pltpu.VMEM((V_WIDTH, QK_WIDTH), F32), pltpu.VMEM((V_WIDTH, QK_WIDTH), F32)],
        compiler_params=_cparams(("parallel", "arbitrary")),
        name="gla",
    )(qkvr, qkvr, qkvr, la, qkvr, qkvr, qkvr, la, *consts)


def _dft_kernel(c_ref, s_ref, f_ref, c128_ref, s128_ref, o_ref, p_acc, q_acc):
    k = pl.program_id(2)

    @pl.when(k == 0)
    def _():
        p_acc[...] = jnp.zeros_like(p_acc)
        q_acc[...] = jnp.zeros_like(q_acc)

    fb = f_ref[...]
    p_acc[...] += _dot(c_ref[...], fb)
    q_acc[...] += _dot(s_ref[...], fb)

    @pl.when(k == pl.num_programs(2) - 1)
    def _():
        c128 = c128_ref[...]
        s128 = s128_ref[...]
        for g in range(FNET_GROUPS):
            sl = slice(g * FNET_GROUP_DIM, (g + 1) * FNET_GROUP_DIM)
            pg = p_acc[:, sl].astype(BF16)
            qg = q_acc[:, sl].astype(BF16)
            o_ref[:, sl] = (_dot(pg, c128) - _dot(qg, s128)).astype(BF16)


def _dft_tables(n, split):
    k = jnp.arange(n, dtype=jnp.int32)
    m_hi = jnp.arange(n // split, dtype=jnp.int32) * split
    m_lo = jnp.arange(split, dtype=jnp.int32)
    w = 2.0 * math.pi / n
    ang_hi = ((m_hi[:, None] * k[None, :]) % n).astype(F32) * w
    ang_lo = ((m_lo[:, None] * k[None, :]) % n).astype(F32) * w
    ch, sh = jnp.cos(ang_hi)[:, None, :], jnp.sin(ang_hi)[:, None, :]
    cl, sl = jnp.cos(ang_lo)[None, :, :], jnp.sin(ang_lo)[None, :, :]
    c = (ch * cl - sh * sl).reshape(n, n)
    s = (sh * cl + ch * sl).reshape(n, n)
    return c.astype(BF16), s.astype(BF16)


def _small_dft_tables(n, scale):
    k = jnp.arange(n, dtype=jnp.int32)
    ang = ((k[:, None] * k[None, :]) % n).astype(F32) * (2.0 * math.pi / n)
    return (jnp.cos(ang) * scale).astype(BF16), (jnp.sin(ang) * scale).astype(BF16)


def _fnet(f, bsz, seq, tm, tk):
    t = bsz * seq
    c_l, s_l = _dft_tables(seq, 64)
    c128, s128 = _small_dft_tables(FNET_GROUP_DIM, (seq * FNET_GROUP_DIM) ** -0.5)
    nm, nk = seq // tm, seq // tk
    return pl.pallas_call(
        _dft_kernel,
        grid=(bsz, nm, nk),
        in_specs=[
            pl.BlockSpec((tm, tk), lambda b, m, k: (m, k)),
            pl.BlockSpec((tm, tk), lambda b, m, k: (m, k)),
            pl.BlockSpec((tk, F_WIDTH), lambda b, m, k: (b * nk + k, 0)),
            _full(c128.shape), _full(s128.shape),
        ],
        out_specs=pl.BlockSpec((tm, F_WIDTH), lambda b, m, k: (b * nm + m, 0)),
        out_shape=jax.ShapeDtypeStruct((t, F_WIDTH), BF16),
        scratch_shapes=[pltpu.VMEM((tm, F_WIDTH), F32), pltpu.VMEM((tm, F_WIDTH), F32)],
        compiler_params=_cparams(("parallel", "parallel", "arbitrary")),
        name="fnet_dft",
    )(c_l, s_l, f, c128, s128)


def _merge_kernel(x_ref, of_ref, ob_ref, r_ref, fr_ref, p_ref,
                  gn_ref, wao_ref, wfo_ref, wmg_ref, bmg_ref, wo_ref, l1g_ref, l1b_ref,
                  wpg_ref, wpe_ref, hb_ref, r2_ref):
    x = x_ref[...]
    xb = x.astype(BF16)
    o = of_ref[...] + ob_ref[...]
    parts = []
    for h in range(GLA_HEADS):
        oh = o[:, h * GLA_DV:(h + 1) * GLA_DV]
        ms = jnp.mean(oh * oh, axis=-1, keepdims=True)
        parts.append(oh * lax.rsqrt(ms + RMS_EPS) * gn_ref[...])
    on = jnp.concatenate(parts, axis=1)
    r = r_ref[...]
    og = (on * (r * jax.nn.sigmoid(r))).astype(BF16)
    branch_gla = _dot(og, wao_ref[...])
    branch_fnet = _dot(fr_ref[...], wfo_ref[...])
    gates = jax.nn.sigmoid(_dot(xb, wmg_ref[...]) + bmg_ref[...])
    merged = gates[:, :D_MODEL] * branch_gla + gates[:, D_MODEL:] * branch_fnet
    mix = _dot(merged.astype(BF16), wo_ref[...])
    h1 = _layer_norm(DEEPNORM_ALPHA * x + mix, l1g_ref[...], l1b_ref[...])
    h1b = h1.astype(BF16)
    hb_ref[...] = h1b
    ple = jax.nn.sigmoid(_dot(h1b, wpg_ref[...])) * _dot(p_ref[...].astype(BF16), wpe_ref[...])
    r2_ref[...] = DEEPNORM_ALPHA * h1 + ple


def _merge(x2, o_f, o_b, qkvr, fr, p2, gn, wao, wfo, wmg, bmg, wo, l1g, l1b, wpg, wpe, tb):
    t = x2.shape[0]
    row = lambda w: pl.BlockSpec((tb, w), lambda i: (i, 0))
    weights = (gn, wao, wfo, wmg, bmg, wo, l1g, l1b, wpg, wpe)
    return pl.pallas_call(
        _merge_kernel,
        grid=(t // tb,),
        in_specs=[row(D_MODEL), row(V_WIDTH), row(V_WIDTH),
                  pl.BlockSpec((tb, V_WIDTH), lambda i: (i, 2)),
                  row(F_WIDTH), row(PLE_DIM)] + [_full(w.shape) for w in weights],
        out_specs=[row(D_MODEL), row(D_MODEL)],
        out_shape=[jax.ShapeDtypeStruct((t, D_MODEL), BF16), jax.ShapeDtypeStruct((t, D_MODEL), F32)],
        compiler_params=_cparams(("parallel",)),
        name="merge_ln1",
    )(x2, o_f, o_b, qkvr, fr, p2, *weights)


def _sort16_pairs():
    def merge(lo, hi, r):
        step = r * 2
        if step < hi - lo:
            yield from merge(lo, hi, step)
            yield from merge(lo + r, hi, step)
            yield from ((i, i + r) for i in range(lo + r, hi - r, step))
        else:
            yield (lo, lo + r)

    def sort(lo, hi):
        if hi - lo >= 1:
            mid = lo + (hi - lo) // 2
            yield from sort(lo, mid)
            yield from sort(mid + 1, hi)
            yield from merge(lo, hi, 1)

    return tuple(sort(0, PEER_TOPK - 1))


def _top16_desc(s):
    assert s.shape[0] == 8 * PEER_TOPK
    v = [s[8 * k:8 * (k + 1), :] for k in range(PEER_TOPK)]
    for i, j in _sort16_pairs():
        v[i], v[j] = jnp.maximum(v[i], v[j]), jnp.minimum(v[i], v[j])
    vals = []
    for r in range(PEER_TOPK):
        m = jnp.max(v[0], axis=0, keepdims=True)
        vals.append(m)
        if r == PEER_TOPK - 1:
            break
        hit = v[0] == m
        last = PEER_TOPK - 1 - r
        for k in range(last):
            v[k] = jnp.where(hit, v[k + 1], v[k])
        v[last] = jnp.where(hit, NEG_INF, v[last])
    return vals


def _count(b, x, strict):
    assert len(b) == 16
    ge = (lambda u: u > x) if strict else (lambda u: u >= x)
    c8 = ge(b[7])
    c4 = ge(jnp.where(c8, b[11], b[3]))
    c2 = ge(jnp.where(c8, jnp.where(c4, b[13], b[9]), jnp.where(c4, b[5], b[1])))
    hi = jnp.where(c4, jnp.where(c2, b[14], b[12]), jnp.where(c2, b[10], b[8]))
    lo = jnp.where(c4, jnp.where(c2, b[6], b[4]), jnp.where(c2, b[2], b[0]))
    c1 = ge(jnp.where(c8, hi, lo))
    pos = (jnp.where(c8, 8.0, 0.0) + jnp.where(c4, 4.0, 0.0)) + (jnp.where(c2, 2.0, 0.0) + jnp.where(c1, 1.0, 0.0))
    return jnp.where(ge(b[15]), 16.0, pos)


def _route_kernel(hb_ref, wpq_ref, keys_ref, rank_ref, e2_ref, n_ref, c1_ref):
    qh = _dot(hb_ref[...], wpq_ref[...]).astype(BF16)
    nk = PEER_NKEYS
    for h in range(PEER_HEADS):
        q1 = qh[:, (2 * h) * nk:(2 * h + 1) * nk]
        q2 = qh[:, (2 * h + 1) * nk:(2 * h + 2) * nk]
        s1 = _dot_nt(keys_ref[2 * h], q1)
        s2 = _dot_nt(keys_ref[2 * h + 1], q2)
        a = _top16_desc(s1)
        b = _top16_desc(s2)
        rank2 = _count(b, s2, strict=True)
        cands = []
        for k in range(PEER_TOPK):
            for l in range(PEER_TOPK // (k + 1)):
                cands.append(a[k] + b[l])
        pad = (-len(cands)) % 8
        cand = jnp.concatenate(cands + [jnp.full_like(a[0], NEG_INF)] * pad, axis=0)
        cur = cand
        for _ in range(PEER_TOPK - 1):
            m = jnp.max(cur, axis=0, keepdims=True)
            cur = jnp.where(cur == m, NEG_INF, cur)
        theta = jnp.max(cur, axis=0, keepdims=True)
        top = a[0] + b[0]
        z = jnp.sum(jnp.where(cand >= theta, jnp.exp(cand - top), 0.0), axis=0, keepdims=True)
        n = _count(b, theta - s1, strict=False)
        rows32 = slice(h * nk // 2, (h + 1) * nk // 2)
        rank_ref[0, rows32, :] = pltpu.bitcast(rank2.astype(BF16), jnp.uint32)
        e2_ref[0, rows32, :] = pltpu.bitcast(jnp.exp(s2 - b[0]).astype(BF16), jnp.uint32)
        tb = n.shape[1]
        n_ref[0, :, h, :, :] = n.reshape(nk // 8, 8, tb)
        c1_ref[0, :, h, :, :] = (jnp.exp(s1 - a[0]) / z).reshape(nk // 8, 8, tb)


def _route(hb, wpq, keys, tb):
    t = hb.shape[0]
    nblk = t // tb
    rows = PEER_HEADS * PEER_NKEYS
    spec = lambda r: pl.BlockSpec((1, r, tb), lambda i: (i, 0, 0))
    shape = lambda r, dt: jax.ShapeDtypeStruct((nblk, r, tb), dt)
    key_dims = (PEER_NKEYS // 8, PEER_HEADS, 8, tb)
    key_spec = pl.BlockSpec((1,) + key_dims, lambda i: (i, 0, 0, 0, 0))
    key_shape = jax.ShapeDtypeStruct((nblk,) + key_dims, F32)
    return pl.pallas_call(
        _route_kernel,
        grid=(nblk,),
        in_specs=[pl.BlockSpec((tb, D_MODEL), lambda i: (i, 0)), _full(wpq.shape), _full(keys.shape)],
        out_specs=[spec(rows // 2), spec(rows // 2), key_spec, key_spec],
        out_shape=[shape(rows // 2, jnp.uint32), shape(rows // 2, jnp.uint32), key_shape, key_shape],
        compiler_params=_cparams(("parallel",)),
        name="peer_route",
    )(hb, wpq, keys)


A_ROWS = 256


def _peer_kernel(hb_ref, rank_ref, e2_ref, n_ref, c1_ref, u_ref, vt_ref, r2_ref, l2g_ref, l2b_ref,
                 y_ref, acc_ref, a_ref, p_ref, *, ec, tb):
    c = pl.program_id(1)
    nk = PEER_NKEYS

    @pl.when(c == 0)
    def _():
        acc_ref[...] = jnp.zeros_like(acc_ref)

    hb = hb_ref[...]
    for r0 in range(0, ec, A_ROWS):
        a_ref[r0:r0 + A_ROWS, :] = _dot_nt(u_ref[r0:r0 + A_ROWS, :], hb)
    for tt in range(tb // LANES):
        cols = slice(tt * LANES, (tt + 1) * LANES)
        n8 = [[n_ref[0, g, h, :, cols] for h in range(PEER_HEADS)] for g in range(ec // (8 * nk))]
        c18 = [[c1_ref[0, g, h, :, cols] for h in range(PEER_HEADS)] for g in range(ec // (8 * nk))]
        for ii in range(ec // nk):
            w = None
            for h in range(PEER_HEADS):
                rows32 = slice(h * nk // 2, (h + 1) * nk // 2)
                rank = pltpu.bitcast(rank_ref[0, rows32, cols], BF16)
                e2 = pltpu.bitcast(e2_ref[0, rows32, cols], BF16)
                n_row = n8[ii // 8][h][ii % 8:ii % 8 + 1, :].astype(BF16)
                c1_row = c18[ii // 8][h][ii % 8:ii % 8 + 1, :].astype(BF16)
                term = c1_row * jnp.where(rank < n_row, e2, jnp.zeros((), BF16))
                w = term if w is None else w + term
            av = a_ref[ii * nk:(ii + 1) * nk, cols].astype(BF16)
            gelu = (0.5 * av) * (1.0 + lax.erf(av * (2.0 ** -0.5)))
            p_ref[ii * nk:(ii + 1) * nk, cols] = gelu * w
    acc_ref[...] += _dot(vt_ref[...], p_ref[...])

    @pl.when(c == pl.num_programs(1) - 1)
    def _():
        ffn = acc_ref[...].T
        y_ref[...] = _layer_norm(r2_ref[...] + ffn, l2g_ref[...], l2b_ref[...])


def _peer(hb, rank2, e2, n, c1, u_b, vt_b, r2, l2g, l2b, tb, ec):
    t = hb.shape[0]
    nblk = t // tb
    rows = PEER_HEADS * PEER_NKEYS
    route_spec = lambda r: pl.BlockSpec((1, r, tb), lambda i, c: (i, 0, 0))
    key_spec = pl.BlockSpec((1, ec // (8 * PEER_NKEYS), PEER_HEADS, 8, tb), lambda i, c: (i, c, 0, 0, 0))
    return pl.pallas_call(
        functools.partial(_peer_kernel, ec=ec, tb=tb),
        grid=(nblk, PEER_EXPERTS // ec),
        in_specs=[pl.BlockSpec((tb, D_MODEL), lambda i, c: (i, 0)),
                  route_spec(rows // 2), route_spec(rows // 2), key_spec, key_spec,
                  pl.BlockSpec((ec, D_MODEL), lambda i, c: (c, 0)),
                  pl.BlockSpec((D_MODEL, ec), lambda i, c: (0, c)),
                  pl.BlockSpec((tb, D_MODEL), lambda i, c: (i, 0)),
                  _full(l2g.shape), _full(l2b.shape)],
        out_specs=pl.BlockSpec((tb, D_MODEL), lambda i, c: (i, 0)),
        out_shape=jax.ShapeDtypeStruct((t, D_MODEL), F32),
        scratch_shapes=[pltpu.VMEM((D_MODEL, tb), F32), pltpu.VMEM((ec, tb), F32), pltpu.VMEM((ec, tb), BF16)],
        compiler_params=_cparams(("parallel", "arbitrary")),
        name="peer_experts",
    )(hb, rank2, e2, n, c1, u_b, vt_b, r2, l2g, l2b)


def _block_sizes(bsz, seq):
    t = bsz * seq
    return dict(
        tb_in=min(512, t),
        tbg=min(512, seq),
        tm=min(1024, seq), tk=min(1024, seq),
        tb_merge=min(256, t),
        tb_peer=min(512, t),
        ec=16 * PEER_NKEYS,
    )


def _prep_weights(w_in, w_af2, b_af, w_ab2, b_ab, gla_norm_g, w_ao, w_fo, w_mg, b_mg, w_o,
                  ln1_g, ln1_b, w_pq, peer_keys, peer_u, peer_v, w_pg, w_pe, ln2_g, ln2_b):
    r = GLA_GATE_RANK
    zeros = jnp.zeros((r, QK_WIDTH), F32)
    w2 = jnp.concatenate([jnp.concatenate([w_af2, zeros], axis=1),
                          jnp.concatenate([zeros, w_ab2], axis=1)], axis=0)
    return dict(
        wq=w_in[:, :OFF_AF].astype(BF16),
        wg=w_in[:, OFF_AF:OFF_F].astype(BF16),
        wf=w_in[:, OFF_F:].astype(BF16),
        w2=w2.astype(BF16),
        b2=jnp.concatenate([b_af, b_ab])[None, :],
        gn=gla_norm_g[None, :],
        wao=w_ao.astype(BF16), wfo=w_fo.astype(BF16), wmg=w_mg.astype(BF16), bmg=b_mg[None, :],
        wo=w_o.astype(BF16), l1g=ln1_g[None, :], l1b=ln1_b[None, :],
        wpq=w_pq.astype(BF16),
        keys=peer_keys.reshape(PEER_HEADS * 2, PEER_NKEYS, PEER_QDIM // 2).astype(BF16),
        u=peer_u.astype(BF16), vt=peer_v.astype(BF16).T,
        wpg=w_pg.astype(BF16), wpe=w_pe.astype(BF16), l2g=ln2_g[None, :], l2b=ln2_b[None, :],
    )


def _encoder_layer(x, p, w):
    bsz, seq, d = x.shape
    t = bsz * seq
    bs = _block_sizes(bsz, seq)
    x2 = x.reshape(t, d)
    p2 = p.reshape(t, PLE_DIM)
    qkvr, la, f = _inproj(x2, w["wq"], w["wg"], w["wf"], w["w2"], w["b2"], bs["tb_in"])
    o_f, o_b = _gla(qkvr, la, bsz, seq, bs["tbg"])
    fr = _fnet(f, bsz, seq, bs["tm"], bs["tk"])
    hb, r2 = _merge(x2, o_f, o_b, qkvr, fr, p2, w["gn"], w["wao"], w["wfo"], w["wmg"], w["bmg"],
                    w["wo"], w["l1g"], w["l1b"], w["wpg"], w["wpe"], bs["tb_merge"])
    tbp = bs["tb_peer"]
    rank2, e2, n, c1 = _route(hb, w["wpq"], w["keys"], tbp)
    y = _peer(hb, rank2, e2, n, c1, w["u"], w["vt"], r2, w["l2g"], w["l2b"], tbp, bs["ec"])
    return y.reshape(bsz, seq, d)


def kernel(x_prompt, x_sample, p_prompt, p_sample, w_in, w_af2, b_af, w_ab2, b_ab, gla_norm_g, w_ao, w_fo, w_mg, b_mg, w_o, ln1_g, ln1_b, w_pq, peer_keys, peer_u, peer_v, w_pg, w_pe, ln2_g, ln2_b):
    params = (w_in, w_af2, b_af, w_ab2, b_ab, gla_norm_g, w_ao, w_fo, w_mg, b_mg, w_o,
              ln1_g, ln1_b, w_pq, peer_keys, peer_u, peer_v, w_pg, w_pe, ln2_g, ln2_b)
    xs = (x_prompt, x_sample)
    ps = (p_prompt, p_sample)
    for i in range(DEPTH):
        w = _prep_weights(*[a[i] for a in params])
        xs = tuple(_encoder_layer(x, p[i], w) for x, p in zip(xs, ps))
    return xs
```

```python
import functools
import math

import jax
import jax.numpy as jnp
from jax import lax
from jax.experimental import pallas as pl
from jax.experimental.pallas import tpu as pltpu

F32 = jnp.float32
BF16 = jnp.bfloat16

D_MODEL = 1024
DEPTH = 1
GLA_HEADS = 4
GLA_DK = 64
GLA_DV = 128
GLA_GATE_RANK = 16
GLA_GATE_NORMALIZER = 16.0
GLA_CHUNK = 64
FNET_GROUPS = 4
FNET_GROUP_DIM = 128
PEER_HEADS = 8
PEER_NKEYS = 128
PEER_EXPERTS = PEER_NKEYS * PEER_NKEYS
PEER_QDIM = 256
PEER_TOPK = 16
PLE_DIM = 256
LN_EPS = 1e-5
RMS_EPS = 1e-6
DEEPNORM_ALPHA = (2.0 * DEPTH) ** 0.25

QK_WIDTH = GLA_HEADS * GLA_DK
V_WIDTH = GLA_HEADS * GLA_DV
F_WIDTH = FNET_GROUPS * FNET_GROUP_DIM
OFF_AF = 2 * QK_WIDTH + 2 * V_WIDTH
OFF_F = OFF_AF + 2 * GLA_GATE_RANK

VMEM_LIMIT_BYTES = 56 * 1024 * 1024
LANES = 128

NEG_INF = float("-inf")


def _cparams(sem, flags=None):
    return pltpu.CompilerParams(dimension_semantics=sem, vmem_limit_bytes=VMEM_LIMIT_BYTES, flags=flags)


def _full(shape):
    nd = len(shape)
    return pl.BlockSpec(shape, lambda *_: (0,) * nd)


def _dot(a, b):
    return jnp.dot(a, b, preferred_element_type=F32)


def _dot_nt(a, b):
    return lax.dot_general(a, b, (((1,), (1,)), ((), ())), preferred_element_type=F32)


def _dot_tn(a, b):
    return lax.dot_general(a, b, (((0,), (0,)), ((), ())), preferred_element_type=F32)


def _layer_norm(x, g, b):
    mu = jnp.mean(x, axis=-1, keepdims=True)
    xc = x - mu
    var = jnp.mean(xc * xc, axis=-1, keepdims=True)
    return xc * lax.rsqrt(var + LN_EPS) * g + b


def _inproj_kernel(x_ref, wq_ref, wg_ref, wf_ref, w2_ref, b2_ref, qkvr_ref, la_ref, f_ref):
    xb = x_ref[...].astype(BF16)
    qkvr_ref[...] = _dot(xb, wq_ref[...])
    g = _dot(xb, wg_ref[...])
    z = _dot(g.astype(BF16), w2_ref[...]) + b2_ref[...]
    log_sig = jnp.minimum(z, 0.0) - jnp.log1p(jnp.exp(-jnp.abs(z)))
    la_ref[...] = log_sig * (1.0 / GLA_GATE_NORMALIZER)
    f_ref[...] = _dot(xb, wf_ref[...]).astype(BF16)


def _inproj(x2, wq, wg, wf, w2, b2, tb):
    t = x2.shape[0]
    nq = wq.shape[1]
    return pl.pallas_call(
        _inproj_kernel,
        grid=(t // tb,),
        in_specs=[
            pl.BlockSpec((tb, D_MODEL), lambda i: (i, 0)),
            _full(wq.shape), _full(wg.shape), _full(wf.shape), _full(w2.shape), _full(b2.shape),
        ],
        out_specs=[
            pl.BlockSpec((tb, nq), lambda i: (i, 0)),
            pl.BlockSpec((tb, 2 * QK_WIDTH), lambda i: (i, 0)),
            pl.BlockSpec((tb, F_WIDTH), lambda i: (i, 0)),
        ],
        out_shape=[
            jax.ShapeDtypeStruct((t, nq), F32),
            jax.ShapeDtypeStruct((t, 2 * QK_WIDTH), F32),
            jax.ShapeDtypeStruct((t, F_WIDTH), BF16),
        ],
        compiler_params=_cparams(("parallel",)),
        name="inproj",
    )(x2, wq, wg, wf, w2, b2)


def _gla_chunk(q, k, v, la, tri, smask, hmask, bmask, st_ref, mid, last):
    c = GLA_CHUNK
    la_hi = la.astype(BF16)
    la_lo = (la - la_hi.astype(F32)).astype(BF16)
    bc = _dot(tri, la_hi) + _dot(tri, la_lo)
    b_mid = bc[mid:mid + 1, :]
    b_last = bc[last:last + 1, :]
    qs = q * (GLA_DK ** -0.5)
    vb = v.astype(BF16)
    qd = qs * jnp.exp(bc - b_mid)
    kd = (k * jnp.exp(b_mid - bc)).astype(BF16)
    qstack = jnp.concatenate(
        [jnp.where(hmask[h:h + 1, :] > 0.0, qd, 0.0) for h in range(GLA_HEADS)], axis=0).astype(BF16)
    scores = _dot_nt(qstack, kd)
    scores = jnp.where(smask > 0.0, scores, 0.0).astype(BF16)
    oi_all = _dot(scores, vb)
    o_intra = jnp.concatenate(
        [oi_all[h * c:(h + 1) * c, h * GLA_DV:(h + 1) * GLA_DV] for h in range(GLA_HEADS)], axis=1)
    q_in = (qs * jnp.exp(bc)).astype(BF16)
    st = st_ref[...]
    o_inter = _dot_nt(q_in, st.astype(BF16))
    k_st = (k * jnp.exp(b_last - bc)).astype(BF16)
    upd = _dot_tn(vb, k_st)
    st_ref[...] = st * jnp.exp(b_last) + jnp.where(bmask > 0.0, upd, 0.0)
    return o_intra + o_inter


def _gla_kernel(qf_ref, kf_ref, vf_ref, laf_ref, qb_ref, kb_ref, vb_ref, lab_ref,
                trif_ref, trib_ref, smf_ref, smb_ref, hm_ref, bm_ref,
                of_ref, ob_ref, stf_ref, stb_ref, *, nchunk):
    @pl.when(pl.program_id(1) == 0)
    def _():
        stf_ref[...] = jnp.zeros_like(stf_ref)
        stb_ref[...] = jnp.zeros_like(stb_ref)

    c = GLA_CHUNK
    trif = trif_ref[...]
    trib = trib_ref[...]
    smf = smf_ref[...]
    smb = smb_ref[...]
    hm = hm_ref[...]
    bm = bm_ref[...]

    def body(j, carry):
        rf = pl.ds(pl.multiple_of(j * c, c), c)
        of_ref[rf, :] = _gla_chunk(qf_ref[rf, :], kf_ref[rf, :], vf_ref[rf, :], laf_ref[rf, :],
                                   trif, smf, hm, bm, stf_ref, c // 2 - 1, c - 1)
        rb = pl.ds(pl.multiple_of((nchunk - 1 - j) * c, c), c)
        ob_ref[rb, :] = _gla_chunk(qb_ref[rb, :], kb_ref[rb, :], vb_ref[rb, :], lab_ref[rb, :],
                                   trib, smb, hm, bm, stb_ref, c // 2, 0)
        return carry

    lax.fori_loop(0, nchunk, body, 0, unroll=4)


def _gla_consts():
    c = GLA_CHUNK
    r = jnp.arange(c)
    trif = (r[:, None] >= r[None, :]).astype(BF16)
    trib = (r[None, :] >= r[:, None]).astype(BF16)
    rr = jnp.arange(GLA_HEADS * c) % c
    smf = (rr[:, None] >= r[None, :]).astype(F32)
    smb = (r[None, :] > rr[:, None]).astype(F32)
    hm = (jnp.arange(QK_WIDTH)[None, :] // GLA_DK == jnp.arange(GLA_HEADS)[:, None]).astype(F32)
    hm = jnp.concatenate([hm, jnp.zeros((8 - GLA_HEADS, QK_WIDTH), F32)], axis=0)
    bm = (jnp.arange(V_WIDTH)[:, None] // GLA_DV == jnp.arange(QK_WIDTH)[None, :] // GLA_DK).astype(F32)
    return trif, trib, smf, smb, hm, bm


def _gla(qkvr, la, bsz, seq, tbg):
    t = bsz * seq
    nblk = seq // tbg
    nchunk = tbg // GLA_CHUNK
    consts = _gla_consts()
    fwd = lambda col: (lambda b, i: (b * nblk + i, col))
    bwd = lambda col: (lambda b, i: (b * nblk + nblk - 1 - i, col))
    in_specs = [
        pl.BlockSpec((tbg, QK_WIDTH), fwd(0)),
        pl.BlockSpec((tbg, QK_WIDTH), fwd(1)),
        pl.BlockSpec((tbg, V_WIDTH), fwd(1)),
        pl.BlockSpec((tbg, QK_WIDTH), fwd(0)),
        pl.BlockSpec((tbg, QK_WIDTH), bwd(0)),
        pl.BlockSpec((tbg, QK_WIDTH), bwd(1)),
        pl.BlockSpec((tbg, V_WIDTH), bwd(1)),
        pl.BlockSpec((tbg, QK_WIDTH), bwd(1)),
    ] + [_full(a.shape) for a in consts]
    return pl.pallas_call(
        functools.partial(_gla_kernel, nchunk=nchunk),
        grid=(bsz, nblk),
        in_specs=in_specs,
        out_specs=[pl.BlockSpec((tbg, V_WIDTH), fwd(0)), pl.BlockSpec((tbg, V_WIDTH), bwd(0))],
        out_shape=[jax.ShapeDtypeStruct((t, V_WIDTH), F32), jax.ShapeDtypeStruct((t, V_WIDTH), F32)],
        scratch_shapes=[pltpu.VMEM((V_WIDTH, QK_WIDTH), F32), pltpu.VMEM((V_WIDTH, QK_WIDTH), F32)],
        compiler_params=_cparams(("parallel", "arbitrary")),
        name="gla",
    )(qkvr, qkvr, qkvr, la, qkvr, qkvr, qkvr, la, *consts)


def _dft_kernel(c_ref, s_ref, f_ref, c128_ref, s128_ref, o_ref, p_acc, q_acc):
    k = pl.program_id(2)

    @pl.when(k == 0)
    def _():
        p_acc[...] = jnp.zeros_like(p_acc)
        q_acc[...] = jnp.zeros_like(q_acc)

    fb = f_ref[...]
    p_acc[...] += _dot(c_ref[...], fb)
    q_acc[...] += _dot(s_ref[...], fb)

    @pl.when(k == pl.num_programs(2) - 1)
    def _():
        c128 = c128_ref[...]
        s128 = s128_ref[...]
        for g in range(FNET_GROUPS):
            sl = slice(g * FNET_GROUP_DIM, (g + 1) * FNET_GROUP_DIM)
            pg = p_acc[:, sl].astype(BF16)
            qg = q_acc[:, sl].astype(BF16)
            o_ref[:, sl] = (_dot(pg, c128) - _dot(qg, s128)).astype(BF16)


def _dft_tables(n, split):
    k = jnp.arange(n, dtype=jnp.int32)
    m_hi = jnp.arange(n // split, dtype=jnp.int32) * split
    m_lo = jnp.arange(split, dtype=jnp.int32)
    w = 2.0 * math.pi / n
    ang_hi = ((m_hi[:, None] * k[None, :]) % n).astype(F32) * w
    ang_lo = ((m_lo[:, None] * k[None, :]) % n).astype(F32) * w
    ch, sh = jnp.cos(ang_hi)[:, None, :], jnp.sin(ang_hi)[:, None, :]
    cl, sl = jnp.cos(ang_lo)[None, :, :], jnp.sin(ang_lo)[None, :, :]
    c = (ch * cl - sh * sl).reshape(n, n)
    s = (sh * cl + ch * sl).reshape(n, n)
    return c.astype(BF16), s.astype(BF16)


def _small_dft_tables(n, scale):
    k = jnp.arange(n, dtype=jnp.int32)
    ang = ((k[:, None] * k[None, :]) % n).astype(F32) * (2.0 * math.pi / n)
    return (jnp.cos(ang) * scale).astype(BF16), (jnp.sin(ang) * scale).astype(BF16)


def _fnet(f, bsz, seq, tm, tk):
    t = bsz * seq
    c_l, s_l = _dft_tables(seq, 64)
    c128, s128 = _small_dft_tables(FNET_GROUP_DIM, (seq * FNET_GROUP_DIM) ** -0.5)
    nm, nk = seq // tm, seq // tk
    return pl.pallas_call(
        _dft_kernel,
        grid=(bsz, nm, nk),
        in_specs=[
            pl.BlockSpec((tm, tk), lambda b, m, k: (m, k)),
            pl.BlockSpec((tm, tk), lambda b, m, k: (m, k)),
            pl.BlockSpec((tk, F_WIDTH), lambda b, m, k: (b * nk + k, 0)),
            _full(c128.shape), _full(s128.shape),
        ],
        out_specs=pl.BlockSpec((tm, F_WIDTH), lambda b, m, k: (b * nm + m, 0)),
        out_shape=jax.ShapeDtypeStruct((t, F_WIDTH), BF16),
        scratch_shapes=[pltpu.VMEM((tm, F_WIDTH), F32), pltpu.VMEM((tm, F_WIDTH), F32)],
        compiler_params=_cparams(("parallel", "parallel", "arbitrary")),
        name="fnet_dft",
    )(c_l, s_l, f, c128, s128)


def _merge_kernel(x_ref, of_ref, ob_ref, r_ref, fr_ref, p_ref,
                  gn_ref, wao_ref, wfo_ref, wmg_ref, bmg_ref, wo_ref, l1g_ref, l1b_ref,
                  wpg_ref, wpe_ref, hb_ref, r2_ref):
    x = x_ref[...]
    xb = x.astype(BF16)
    o = of_ref[...] + ob_ref[...]
    parts = []
    for h in range(GLA_HEADS):
        oh = o[:, h * GLA_DV:(h + 1) * GLA_DV]
        ms = jnp.mean(oh * oh, axis=-1, keepdims=True)
        parts.append(oh * lax.rsqrt(ms + RMS_EPS) * gn_ref[...])
    on = jnp.concatenate(parts, axis=1)
    r = r_ref[...]
    og = (on * (r * jax.nn.sigmoid(r))).astype(BF16)
    branch_gla = _dot(og, wao_ref[...])
    branch_fnet = _dot(fr_ref[...], wfo_ref[...])
    gates = jax.nn.sigmoid(_dot(xb, wmg_ref[...]) + bmg_ref[...])
    merged = gates[:, :D_MODEL] * branch_gla + gates[:, D_MODEL:] * branch_fnet
    mix = _dot(merged.astype(BF16), wo_ref[...])
    h1 = _layer_norm(DEEPNORM_ALPHA * x + mix, l1g_ref[...], l1b_ref[...])
    h1b = h1.astype(BF16)
    hb_ref[...] = h1b
    ple = jax.nn.sigmoid(_dot(h1b, wpg_ref[...])) * _dot(p_ref[...].astype(BF16), wpe_ref[...])
    r2_ref[...] = DEEPNORM_ALPHA * h1 + ple


def _merge(x2, o_f, o_b, qkvr, fr, p2, gn, wao, wfo, wmg, bmg, wo, l1g, l1b, wpg, wpe, tb):
    t = x2.shape[0]
    row = lambda w: pl.BlockSpec((tb, w), lambda i: (i, 0))
    weights = (gn, wao, wfo, wmg, bmg, wo, l1g, l1b, wpg, wpe)
    return pl.pallas_call(
        _merge_kernel,
        grid=(t // tb,),
        in_specs=[row(D_MODEL), row(V_WIDTH), row(V_WIDTH),
                  pl.BlockSpec((tb, V_WIDTH), lambda i: (i, 2)),
                  row(F_WIDTH), row(PLE_DIM)] + [_full(w.shape) for w in weights],
        out_specs=[row(D_MODEL), row(D_MODEL)],
        out_shape=[jax.ShapeDtypeStruct((t, D_MODEL), BF16), jax.ShapeDtypeStruct((t, D_MODEL), F32)],
        compiler_params=_cparams(("parallel",)),
        name="merge_ln1",
    )(x2, o_f, o_b, qkvr, fr, p2, *weights)


def _sort16_pairs():
    def merge(lo, hi, r):
        step = r * 2
        if step < hi - lo:
            yield from merge(lo, hi, step)
            yield from merge(lo + r, hi, step)
            yield from ((i, i + r) for i in range(lo + r, hi - r, step))
        else:
            yield (lo, lo + r)

    def sort(lo, hi):
        if hi - lo >= 1:
            mid = lo + (hi - lo) // 2
            yield from sort(lo, mid)
            yield from sort(mid + 1, hi)
            yield from merge(lo, hi, 1)

    return tuple(sort(0, PEER_TOPK - 1))


def _top16_desc(s):
    assert s.shape[0] == 8 * PEER_TOPK
    v = [s[8 * k:8 * (k + 1), :] for k in range(PEER_TOPK)]
    for i, j in _sort16_pairs():
        v[i], v[j] = jnp.maximum(v[i], v[j]), jnp.minimum(v[i], v[j])
    vals = []
    for r in range(PEER_TOPK):
        m = jnp.max(v[0], axis=0, keepdims=True)
        vals.append(m)
        if r == PEER_TOPK - 1:
            break
        hit = v[0] == m
        last = PEER_TOPK - 1 - r
        for k in range(last):
            v[k] = jnp.where(hit, v[k + 1], v[k])
        v[last] = jnp.where(hit, NEG_INF, v[last])
    return vals


def _count(b, x, strict):
    assert len(b) == 16
    ge = (lambda u: u > x) if strict else (lambda u: u >= x)
    c8 = ge(b[7])
    c4 = ge(jnp.where(c8, b[11], b[3]))
    c2 = ge(jnp.where(c8, jnp.where(c4, b[13], b[9]), jnp.where(c4, b[5], b[1])))
    hi = jnp.where(c4, jnp.where(c2, b[14], b[12]), jnp.where(c2, b[10], b[8]))
    lo = jnp.where(c4, jnp.where(c2, b[6], b[4]), jnp.where(c2, b[2], b[0]))
    c1 = ge(jnp.where(c8, hi, lo))
    pos = (jnp.where(c8, 8.0, 0.0) + jnp.where(c4, 4.0, 0.0)) + (jnp.where(c2, 2.0, 0.0) + jnp.where(c1, 1.0, 0.0))
    return jnp.where(ge(b[15]), 16.0, pos)


def _route_kernel(hb_ref, wpq_ref, keys_ref, rank_ref, e2_ref, n_ref, c1_ref):
    qh = _dot(hb_ref[...], wpq_ref[...]).astype(BF16)
    nk = PEER_NKEYS
    for h in range(PEER_HEADS):
        q1 = qh[:, (2 * h) * nk:(2 * h + 1) * nk]
        q2 = qh[:, (2 * h + 1) * nk:(2 * h + 2) * nk]
        s1 = _dot_nt(keys_ref[2 * h], q1)
        s2 = _dot_nt(keys_ref[2 * h + 1], q2)
        a = _top16_desc(s1)
        b = _top16_desc(s2)
        rank2 = _count(b, s2, strict=True)
        cands = []
        for k in range(PEER_TOPK):
            for l in range(PEER_TOPK // (k + 1)):
                cands.append(a[k] + b[l])
        pad = (-len(cands)) % 8
        cand = jnp.concatenate(cands + [jnp.full_like(a[0], NEG_INF)] * pad, axis=0)
        cur = cand
        for _ in range(PEER_TOPK - 1):
            m = jnp.max(cur, axis=0, keepdims=True)
            cur = jnp.where(cur == m, NEG_INF, cur)
        theta = jnp.max(cur, axis=0, keepdims=True)
        top = a[0] + b[0]
        z = jnp.sum(jnp.where(cand >= theta, jnp.exp(cand - top), 0.0), axis=0, keepdims=True)
        n = _count(b, theta - s1, strict=False)
        rows32 = slice(h * nk // 2, (h + 1) * nk // 2)
        rank_ref[0, rows32, :] = pltpu.bitcast(rank2.astype(BF16), jnp.uint32)
        e2_ref[0, rows32, :] = pltpu.bitcast(jnp.exp(s2 - b[0]).astype(BF16), jnp.uint32)
        tb = n.shape[1]
        n_ref[0, :, h, :, :] = n.reshape(nk // 8, 8, tb)
        c1_ref[0, :, h, :, :] = (jnp.exp(s1 - a[0]) / z).reshape(nk // 8, 8, tb)


def _route(hb, wpq, keys, tb):
    t = hb.shape[0]
    nblk = t // tb
    rows = PEER_HEADS * PEER_NKEYS
    spec = lambda r: pl.BlockSpec((1, r, tb), lambda i: (i, 0, 0))
    shape = lambda r, dt: jax.ShapeDtypeStruct((nblk, r, tb), dt)
    key_dims = (PEER_NKEYS // 8, PEER_HEADS, 8, tb)
    key_spec = pl.BlockSpec((1,) + key_dims, lambda i: (i, 0, 0, 0, 0))
    key_shape = jax.ShapeDtypeStruct((nblk,) + key_dims, F32)
    return pl.pallas_call(
        _route_kernel,
        grid=(nblk,),
        in_specs=[pl.BlockSpec((tb, D_MODEL), lambda i: (i, 0)), _full(wpq.shape), _full(keys.shape)],
        out_specs=[spec(rows // 2), spec(rows // 2), key_spec, key_spec],
        out_shape=[shape(rows // 2, jnp.uint32), shape(rows // 2, jnp.uint32), key_shape, key_shape],
        compiler_params=_cparams(("parallel",)),
        name="peer_route",
    )(hb, wpq, keys)


A_ROWS = 256


def _peer_kernel(hb_ref, rank_ref, e2_ref, n_ref, c1_ref, u_ref, vt_ref, r2_ref, l2g_ref, l2b_ref,
                 y_ref, acc_ref, a_ref, p_ref, *, ec, tb):
    c = pl.program_id(1)
    nk = PEER_NKEYS

    @pl.when(c == 0)
    def _():
        acc_ref[...] = jnp.zeros_like(acc_ref)

    hb = hb_ref[...]
    for r0 in range(0, ec, A_ROWS):
        a_ref[r0:r0 + A_ROWS, :] = _dot_nt(u_ref[r0:r0 + A_ROWS, :], hb)
    tbr = rank_ref.shape[2]
    for tt in range(tb // LANES):
        cols = slice(tt * LANES, (tt + 1) * LANES)
        rb = tt * LANES // tbr
        rcols = slice(tt * LANES % tbr, tt * LANES % tbr + LANES)
        n8 = [[n_ref[rb, g, h, :, rcols] for h in range(PEER_HEADS)] for g in range(ec // (8 * nk))]
        c18 = [[c1_ref[rb, g, h, :, rcols] for h in range(PEER_HEADS)] for g in range(ec // (8 * nk))]
        for ii in range(ec // nk):
            w = None
            for h in range(PEER_HEADS):
                rows32 = slice(h * nk // 2, (h + 1) * nk // 2)
                rank = pltpu.bitcast(rank_ref[rb, rows32, rcols], BF16)
                e2 = pltpu.bitcast(e2_ref[rb, rows32, rcols], BF16)
                n_row = n8[ii // 8][h][ii % 8:ii % 8 + 1, :].astype(BF16)
                c1_row = c18[ii // 8][h][ii % 8:ii % 8 + 1, :].astype(BF16)
                term = c1_row * jnp.where(rank < n_row, e2, jnp.zeros((), BF16))
                w = term if w is None else w + term
            av = a_ref[ii * nk:(ii + 1) * nk, cols].astype(BF16)
            gelu = (0.5 * av) * (1.0 + lax.erf(av * (2.0 ** -0.5)))
            p_ref[ii * nk:(ii + 1) * nk, cols] = gelu * w
    acc_ref[...] += _dot(vt_ref[...], p_ref[...])

    @pl.when(c == pl.num_programs(1) - 1)
    def _():
        ffn = acc_ref[...].T
        y_ref[...] = _layer_norm(r2_ref[...] + ffn, l2g_ref[...], l2b_ref[...])


def _peer(hb, rank2, e2, n, c1, u_b, vt_b, r2, l2g, l2b, tb, ec):
    t = hb.shape[0]
    nblk = t // tb
    rows = PEER_HEADS * PEER_NKEYS
    tbr = rank2.shape[2]
    nrb = tb // tbr
    route_spec = lambda r: pl.BlockSpec((nrb, r, tbr), lambda i, c: (i, 0, 0))
    key_spec = pl.BlockSpec((nrb, ec // (8 * PEER_NKEYS), PEER_HEADS, 8, tbr), lambda i, c: (i, c, 0, 0, 0))
    return pl.pallas_call(
        functools.partial(_peer_kernel, ec=ec, tb=tb),
        grid=(nblk, PEER_EXPERTS // ec),
        in_specs=[pl.BlockSpec((tb, D_MODEL), lambda i, c: (i, 0)),
                  route_spec(rows // 2), route_spec(rows // 2), key_spec, key_spec,
                  pl.BlockSpec((ec, D_MODEL), lambda i, c: (c, 0)),
                  pl.BlockSpec((D_MODEL, ec), lambda i, c: (0, c)),
                  pl.BlockSpec((tb, D_MODEL), lambda i, c: (i, 0)),
                  _full(l2g.shape), _full(l2b.shape)],
        out_specs=pl.BlockSpec((tb, D_MODEL), lambda i, c: (i, 0)),
        out_shape=jax.ShapeDtypeStruct((t, D_MODEL), F32),
        scratch_shapes=[pltpu.VMEM((D_MODEL, tb), F32), pltpu.VMEM((ec, tb), F32), pltpu.VMEM((ec, tb), BF16)],
        compiler_params=_cparams(("parallel", "arbitrary")),
        name="peer_experts",
    )(hb, rank2, e2, n, c1, u_b, vt_b, r2, l2g, l2b)


def _block_sizes(bsz, seq):
    t = bsz * seq
    return dict(
        tb_in=min(512, t),
        tbg=min(512, seq),
        tm=min(1024, seq), tk=min(1024, seq),
        tb_merge=min(256, t),
        tb_route=min(512, t),
        tb_peer=min(1024, t),
        ec=8 * PEER_NKEYS,
    )


def _prep_weights(w_in, w_af2, b_af, w_ab2, b_ab, gla_norm_g, w_ao, w_fo, w_mg, b_mg, w_o,
                  ln1_g, ln1_b, w_pq, peer_keys, peer_u, peer_v, w_pg, w_pe, ln2_g, ln2_b):
    r = GLA_GATE_RANK
    zeros = jnp.zeros((r, QK_WIDTH), F32)
    w2 = jnp.concatenate([jnp.concatenate([w_af2, zeros], axis=1),
                          jnp.concatenate([zeros, w_ab2], axis=1)], axis=0)
    return dict(
        wq=w_in[:, :OFF_AF].astype(BF16),
        wg=w_in[:, OFF_AF:OFF_F].astype(BF16),
        wf=w_in[:, OFF_F:].astype(BF16),
        w2=w2.astype(BF16),
        b2=jnp.concatenate([b_af, b_ab])[None, :],
        gn=gla_norm_g[None, :],
        wao=w_ao.astype(BF16), wfo=w_fo.astype(BF16), wmg=w_mg.astype(BF16), bmg=b_mg[None, :],
        wo=w_o.astype(BF16), l1g=ln1_g[None, :], l1b=ln1_b[None, :],
        wpq=w_pq.astype(BF16),
        keys=peer_keys.reshape(PEER_HEADS * 2, PEER_NKEYS, PEER_QDIM // 2).astype(BF16),
        u=peer_u.astype(BF16), vt=peer_v.astype(BF16).T,
        wpg=w_pg.astype(BF16), wpe=w_pe.astype(BF16), l2g=ln2_g[None, :], l2b=ln2_b[None, :],
    )


def _encoder_layer(x, p, w):
    bsz, seq, d = x.shape
    t = bsz * seq
    bs = _block_sizes(bsz, seq)
    x2 = x.reshape(t, d)
    p2 = p.reshape(t, PLE_DIM)
    qkvr, la, f = _inproj(x2, w["wq"], w["wg"], w["wf"], w["w2"], w["b2"], bs["tb_in"])
    o_f, o_b = _gla(qkvr, la, bsz, seq, bs["tbg"])
    fr = _fnet(f, bsz, seq, bs["tm"], bs["tk"])
    hb, r2 = _merge(x2, o_f, o_b, qkvr, fr, p2, w["gn"], w["wao"], w["wfo"], w["wmg"], w["bmg"],
                    w["wo"], w["l1g"], w["l1b"], w["wpg"], w["wpe"], bs["tb_merge"])
    rank2, e2, n, c1 = _route(hb, w["wpq"], w["keys"], bs["tb_route"])
    y = _peer(hb, rank2, e2, n, c1, w["u"], w["vt"], r2, w["l2g"], w["l2b"], bs["tb_peer"], bs["ec"])
    return y.reshape(bsz, seq, d)


def kernel(x_prompt, x_sample, p_prompt, p_sample, w_in, w_af2, b_af, w_ab2, b_ab, gla_norm_g, w_ao, w_fo, w_mg, b_mg, w_o, ln1_g, ln1_b, w_pq, peer_keys, peer_u, peer_v, w_pg, w_pe, ln2_g, ln2_b):
    params = (w_in, w_af2, b_af, w_ab2, b_ab, gla_norm_g, w_ao, w_fo, w_mg, b_mg, w_o,
              ln1_g, ln1_b, w_pq, peer_keys, peer_u, peer_v, w_pg, w_pe, ln2_g, ln2_b)
    xs = (x_prompt, x_sample)
    ps = (p_prompt, p_sample)
    for i in range(DEPTH):
        w = _prep_weights(*[a[i] for a in params])
        xs = tuple(_encoder_layer(x, p[i], w) for x, p in zip(xs, ps))
    return xs
```

```python
import functools
import math

import jax
import jax.numpy as jnp
from jax import lax
from jax.experimental import pallas as pl
from jax.experimental.pallas import tpu as pltpu

F32 = jnp.float32
BF16 = jnp.bfloat16

D_MODEL = 1024
DEPTH = 1
GLA_HEADS = 4
GLA_DK = 64
GLA_DV = 128
GLA_GATE_RANK = 16
GLA_GATE_NORMALIZER = 16.0
GLA_CHUNK = 64
FNET_GROUPS = 4
FNET_GROUP_DIM = 128
PEER_HEADS = 8
PEER_NKEYS = 128
PEER_EXPERTS = PEER_NKEYS * PEER_NKEYS
PEER_QDIM = 256
PEER_TOPK = 16
PLE_DIM = 256
LN_EPS = 1e-5
RMS_EPS = 1e-6
DEEPNORM_ALPHA = (2.0 * DEPTH) ** 0.25

QK_WIDTH = GLA_HEADS * GLA_DK
V_WIDTH = GLA_HEADS * GLA_DV
F_WIDTH = FNET_GROUPS * FNET_GROUP_DIM
OFF_AF = 2 * QK_WIDTH + 2 * V_WIDTH
OFF_F = OFF_AF + 2 * GLA_GATE_RANK

VMEM_LIMIT_BYTES = 56 * 1024 * 1024
LANES = 128

NEG_INF = float("-inf")


def _cparams(sem, flags=None):
    return pltpu.CompilerParams(dimension_semantics=sem, vmem_limit_bytes=VMEM_LIMIT_BYTES, flags=flags)


def _full(shape):
    nd = len(shape)
    return pl.BlockSpec(shape, lambda *_: (0,) * nd)


def _dot(a, b):
    return jnp.dot(a, b, preferred_element_type=F32)


def _dot_nt(a, b):
    return lax.dot_general(a, b, (((1,), (1,)), ((), ())), preferred_element_type=F32)


def _dot_tn(a, b):
    return lax.dot_general(a, b, (((0,), (0,)), ((), ())), preferred_element_type=F32)


def _layer_norm(x, g, b):
    mu = jnp.mean(x, axis=-1, keepdims=True)
    xc = x - mu
    var = jnp.mean(xc * xc, axis=-1, keepdims=True)
    return xc * lax.rsqrt(var + LN_EPS) * g + b


def _inproj_kernel(x_ref, wq_ref, wg_ref, wf_ref, w2_ref, b2_ref, qkvr_ref, la_ref, f_ref):
    xb = x_ref[...].astype(BF16)
    qkvr_ref[...] = _dot(xb, wq_ref[...])
    g = _dot(xb, wg_ref[...])
    z = _dot(g.astype(BF16), w2_ref[...]) + b2_ref[...]
    log_sig = jnp.minimum(z, 0.0) - jnp.log1p(jnp.exp(-jnp.abs(z)))
    la_ref[...] = log_sig * (1.0 / GLA_GATE_NORMALIZER)
    f_ref[...] = _dot(xb, wf_ref[...]).astype(BF16)


def _inproj(x2, wq, wg, wf, w2, b2, tb):
    t = x2.shape[0]
    nq = wq.shape[1]
    return pl.pallas_call(
        _inproj_kernel,
        grid=(t // tb,),
        in_specs=[
            pl.BlockSpec((tb, D_MODEL), lambda i: (i, 0)),
            _full(wq.shape), _full(wg.shape), _full(wf.shape), _full(w2.shape), _full(b2.shape),
        ],
        out_specs=[
            pl.BlockSpec((tb, nq), lambda i: (i, 0)),
            pl.BlockSpec((tb, 2 * QK_WIDTH), lambda i: (i, 0)),
            pl.BlockSpec((tb, F_WIDTH), lambda i: (i, 0)),
        ],
        out_shape=[
            jax.ShapeDtypeStruct((t, nq), F32),
            jax.ShapeDtypeStruct((t, 2 * QK_WIDTH), F32),
            jax.ShapeDtypeStruct((t, F_WIDTH), BF16),
        ],
        compiler_params=_cparams(("parallel",)),
        name="inproj",
    )(x2, wq, wg, wf, w2, b2)


def _gla_chunk(q, k, v, la, tri, smask, hmask, bmask, st_ref, mid, last):
    c = GLA_CHUNK
    la_hi = la.astype(BF16)
    la_lo = (la - la_hi.astype(F32)).astype(BF16)
    bc = _dot(tri, la_hi) + _dot(tri, la_lo)
    b_mid = bc[mid:mid + 1, :]
    b_last = bc[last:last + 1, :]
    qs = q * (GLA_DK ** -0.5)
    vb = v.astype(BF16)
    qd = qs * jnp.exp(bc - b_mid)
    kd = (k * jnp.exp(b_mid - bc)).astype(BF16)
    qstack = jnp.concatenate(
        [jnp.where(hmask[h:h + 1, :] > 0.0, qd, 0.0) for h in range(GLA_HEADS)], axis=0).astype(BF16)
    scores = _dot_nt(qstack, kd)
    scores = jnp.where(smask > 0.0, scores, 0.0).astype(BF16)
    oi_all = _dot(scores, vb)
    o_intra = jnp.concatenate(
        [oi_all[h * c:(h + 1) * c, h * GLA_DV:(h + 1) * GLA_DV] for h in range(GLA_HEADS)], axis=1)
    q_in = (qs * jnp.exp(bc)).astype(BF16)
    st = st_ref[...]
    o_inter = _dot_nt(q_in, st.astype(BF16))
    k_st = (k * jnp.exp(b_last - bc)).astype(BF16)
    upd = _dot_tn(vb, k_st)
    st_ref[...] = st * jnp.exp(b_last) + jnp.where(bmask > 0.0, upd, 0.0)
    return o_intra + o_inter


def _gla_kernel(qf_ref, kf_ref, vf_ref, laf_ref, qb_ref, kb_ref, vb_ref, lab_ref,
                trif_ref, trib_ref, smf_ref, smb_ref, hm_ref, bm_ref,
                of_ref, ob_ref, stf_ref, stb_ref, *, nchunk):
    @pl.when(pl.program_id(1) == 0)
    def _():
        stf_ref[...] = jnp.zeros_like(stf_ref)
        stb_ref[...] = jnp.zeros_like(stb_ref)

    c = GLA_CHUNK
    trif = trif_ref[...]
    trib = trib_ref[...]
    smf = smf_ref[...]
    smb = smb_ref[...]
    hm = hm_ref[...]
    bm = bm_ref[...]

    def body(j, carry):
        rf = pl.ds(pl.multiple_of(j * c, c), c)
        of_ref[rf, :] = _gla_chunk(qf_ref[rf, :], kf_ref[rf, :], vf_ref[rf, :], laf_ref[rf, :],
                                   trif, smf, hm, bm, stf_ref, c // 2 - 1, c - 1)
        rb = pl.ds(pl.multiple_of((nchunk - 1 - j) * c, c), c)
        ob_ref[rb, :] = _gla_chunk(qb_ref[rb, :], kb_ref[rb, :], vb_ref[rb, :], lab_ref[rb, :],
                                   trib, smb, hm, bm, stb_ref, c // 2, 0)
        return carry

    lax.fori_loop(0, nchunk, body, 0, unroll=8)


def _gla_consts():
    c = GLA_CHUNK
    r = jnp.arange(c)
    trif = (r[:, None] >= r[None, :]).astype(BF16)
    trib = (r[None, :] >= r[:, None]).astype(BF16)
    rr = jnp.arange(GLA_HEADS * c) % c
    smf = (rr[:, None] >= r[None, :]).astype(F32)
    smb = (r[None, :] > rr[:, None]).astype(F32)
    hm = (jnp.arange(QK_WIDTH)[None, :] // GLA_DK == jnp.arange(GLA_HEADS)[:, None]).astype(F32)
    hm = jnp.concatenate([hm, jnp.zeros((8 - GLA_HEADS, QK_WIDTH), F32)], axis=0)
    bm = (jnp.arange(V_WIDTH)[:, None] // GLA_DV == jnp.arange(QK_WIDTH)[None, :] // GLA_DK).astype(F32)
    return trif, trib, smf, smb, hm, bm


def _gla(qkvr, la, bsz, seq, tbg):
    t = bsz * seq
    nblk = seq // tbg
    nchunk = tbg // GLA_CHUNK
    consts = _gla_consts()
    fwd = lambda col: (lambda b, i: (b * nblk + i, col))
    bwd = lambda col: (lambda b, i: (b * nblk + nblk - 1 - i, col))
    in_specs = [
        pl.BlockSpec((tbg, QK_WIDTH), fwd(0)),
        pl.BlockSpec((tbg, QK_WIDTH), fwd(1)),
        pl.BlockSpec((tbg, V_WIDTH), fwd(1)),
        pl.BlockSpec((tbg, QK_WIDTH), fwd(0)),
        pl.BlockSpec((tbg, QK_WIDTH), bwd(0)),
        pl.BlockSpec((tbg, QK_WIDTH), bwd(1)),
        pl.BlockSpec((tbg, V_WIDTH), bwd(1)),
        pl.BlockSpec((tbg, QK_WIDTH), bwd(1)),
    ] + [_full(a.shape) for a in consts]
    return pl.pallas_call(
        functools.partial(_gla_kernel, nchunk=nchunk),
        grid=(bsz, nblk),
        in_specs=in_specs,
        out_specs=[pl.BlockSpec((tbg, V_WIDTH), fwd(0)), pl.BlockSpec((tbg, V_WIDTH), bwd(0))],
        out_shape=[jax.ShapeDtypeStruct((t, V_WIDTH), F32), jax.ShapeDtypeStruct((t, V_WIDTH), F32)],
        scratch_shapes=[pltpu.VMEM((V_WIDTH, QK_WIDTH), F32), pltpu.VMEM((V_WIDTH, QK_WIDTH), F32)],
        compiler_params=_cparams(("parallel", "arbitrary")),
        name="gla",
    )(qkvr, qkvr, qkvr, la, qkvr, qkvr, qkvr, la, *consts)


def _dft_kernel(c_ref, s_ref, f_ref, c128_ref, s128_ref, o_ref, p_acc, q_acc):
    k = pl.program_id(2)

    @pl.when(k == 0)
    def _():
        p_acc[...] = jnp.zeros_like(p_acc)
        q_acc[...] = jnp.zeros_like(q_acc)

    fb = f_ref[...]
    p_acc[...] += _dot(c_ref[...], fb)
    q_acc[...] += _dot(s_ref[...], fb)

    @pl.when(k == pl.num_programs(2) - 1)
    def _():
        c128 = c128_ref[...]
        s128 = s128_ref[...]
        for g in range(FNET_GROUPS):
            sl = slice(g * FNET_GROUP_DIM, (g + 1) * FNET_GROUP_DIM)
            pg = p_acc[:, sl].astype(BF16)
            qg = q_acc[:, sl].astype(BF16)
            o_ref[:, sl] = (_dot(pg, c128) - _dot(qg, s128)).astype(BF16)


def _dft_tables(n, split):
    k = jnp.arange(n, dtype=jnp.int32)
    m_hi = jnp.arange(n // split, dtype=jnp.int32) * split
    m_lo = jnp.arange(split, dtype=jnp.int32)
    w = 2.0 * math.pi / n
    ang_hi = ((m_hi[:, None] * k[None, :]) % n).astype(F32) * w
    ang_lo = ((m_lo[:, None] * k[None, :]) % n).astype(F32) * w
    ch, sh = jnp.cos(ang_hi)[:, None, :], jnp.sin(ang_hi)[:, None, :]
    cl, sl = jnp.cos(ang_lo)[None, :, :], jnp.sin(ang_lo)[None, :, :]
    c = (ch * cl - sh * sl).reshape(n, n)
    s = (sh * cl + ch * sl).reshape(n, n)
    return c.astype(BF16), s.astype(BF16)


def _small_dft_tables(n, scale):
    k = jnp.arange(n, dtype=jnp.int32)
    ang = ((k[:, None] * k[None, :]) % n).astype(F32) * (2.0 * math.pi / n)
    return (jnp.cos(ang) * scale).astype(BF16), (jnp.sin(ang) * scale).astype(BF16)


def _fnet(f, bsz, seq, tm, tk):
    t = bsz * seq
    c_l, s_l = _dft_tables(seq, 64)
    c128, s128 = _small_dft_tables(FNET_GROUP_DIM, (seq * FNET_GROUP_DIM) ** -0.5)
    nm, nk = seq // tm, seq // tk
    return pl.pallas_call(
        _dft_kernel,
        grid=(bsz, nm, nk),
        in_specs=[
            pl.BlockSpec((tm, tk), lambda b, m, k: (m, k)),
            pl.BlockSpec((tm, tk), lambda b, m, k: (m, k)),
            pl.BlockSpec((tk, F_WIDTH), lambda b, m, k: (b * nk + k, 0)),
            _full(c128.shape), _full(s128.shape),
        ],
        out_specs=pl.BlockSpec((tm, F_WIDTH), lambda b, m, k: (b * nm + m, 0)),
        out_shape=jax.ShapeDtypeStruct((t, F_WIDTH), BF16),
        scratch_shapes=[pltpu.VMEM((tm, F_WIDTH), F32), pltpu.VMEM((tm, F_WIDTH), F32)],
        compiler_params=_cparams(("parallel", "parallel", "arbitrary")),
        name="fnet_dft",
    )(c_l, s_l, f, c128, s128)


def _merge_kernel(x_ref, of_ref, ob_ref, r_ref, fr_ref, p_ref,
                  gn_ref, wao_ref, wfo_ref, wmg_ref, bmg_ref, wo_ref, l1g_ref, l1b_ref,
                  wpg_ref, wpe_ref, hb_ref, r2_ref):
    x = x_ref[...]
    xb = x.astype(BF16)
    o = of_ref[...] + ob_ref[...]
    parts = []
    for h in range(GLA_HEADS):
        oh = o[:, h * GLA_DV:(h + 1) * GLA_DV]
        ms = jnp.mean(oh * oh, axis=-1, keepdims=True)
        parts.append(oh * lax.rsqrt(ms + RMS_EPS) * gn_ref[...])
    on = jnp.concatenate(parts, axis=1)
    r = r_ref[...]
    og = (on * (r * jax.nn.sigmoid(r))).astype(BF16)
    branch_gla = _dot(og, wao_ref[...])
    branch_fnet = _dot(fr_ref[...], wfo_ref[...])
    gates = jax.nn.sigmoid(_dot(xb, wmg_ref[...]) + bmg_ref[...])
    merged = gates[:, :D_MODEL] * branch_gla + gates[:, D_MODEL:] * branch_fnet
    mix = _dot(merged.astype(BF16), wo_ref[...])
    h1 = _layer_norm(DEEPNORM_ALPHA * x + mix, l1g_ref[...], l1b_ref[...])
    h1b = h1.astype(BF16)
    hb_ref[...] = h1b
    ple = jax.nn.sigmoid(_dot(h1b, wpg_ref[...])) * _dot(p_ref[...].astype(BF16), wpe_ref[...])
    r2_ref[...] = DEEPNORM_ALPHA * h1 + ple


def _merge(x2, o_f, o_b, qkvr, fr, p2, gn, wao, wfo, wmg, bmg, wo, l1g, l1b, wpg, wpe, tb):
    t = x2.shape[0]
    row = lambda w: pl.BlockSpec((tb, w), lambda i: (i, 0))
    weights = (gn, wao, wfo, wmg, bmg, wo, l1g, l1b, wpg, wpe)
    return pl.pallas_call(
        _merge_kernel,
        grid=(t // tb,),
        in_specs=[row(D_MODEL), row(V_WIDTH), row(V_WIDTH),
                  pl.BlockSpec((tb, V_WIDTH), lambda i: (i, 2)),
                  row(F_WIDTH), row(PLE_DIM)] + [_full(w.shape) for w in weights],
        out_specs=[row(D_MODEL), row(D_MODEL)],
        out_shape=[jax.ShapeDtypeStruct((t, D_MODEL), BF16), jax.ShapeDtypeStruct((t, D_MODEL), F32)],
        compiler_params=_cparams(("parallel",)),
        name="merge_ln1",
    )(x2, o_f, o_b, qkvr, fr, p2, *weights)


def _sort16_pairs():
    def merge(lo, hi, r):
        step = r * 2
        if step < hi - lo:
            yield from merge(lo, hi, step)
            yield from merge(lo + r, hi, step)
            yield from ((i, i + r) for i in range(lo + r, hi - r, step))
        else:
            yield (lo, lo + r)

    def sort(lo, hi):
        if hi - lo >= 1:
            mid = lo + (hi - lo) // 2
            yield from sort(lo, mid)
            yield from sort(mid + 1, hi)
            yield from merge(lo, hi, 1)

    return tuple(sort(0, PEER_TOPK - 1))


def _top16_desc(s):
    assert s.shape[0] == 8 * PEER_TOPK
    v = [s[8 * k:8 * (k + 1), :] for k in range(PEER_TOPK)]
    for i, j in _sort16_pairs():
        v[i], v[j] = jnp.maximum(v[i], v[j]), jnp.minimum(v[i], v[j])
    vals = []
    for r in range(PEER_TOPK):
        m = jnp.max(v[0], axis=0, keepdims=True)
        vals.append(m)
        if r == PEER_TOPK - 1:
            break
        hit = v[0] == m
        last = PEER_TOPK - 1 - r
        for k in range(last):
            v[k] = jnp.where(hit, v[k + 1], v[k])
        v[last] = jnp.where(hit, NEG_INF, v[last])
    return vals


def _bf16_twice(x):
    u = pltpu.bitcast(x.astype(BF16).astype(F32), jnp.uint32)
    return u | lax.shift_right_logical(u, jnp.uint32(16))


def _count(b, x, strict):
    assert len(b) == 16
    ge = (lambda u: u > x) if strict else (lambda u: u >= x)
    c8 = ge(b[7])
    c4 = ge(jnp.where(c8, b[11], b[3]))
    c2 = ge(jnp.where(c8, jnp.where(c4, b[13], b[9]), jnp.where(c4, b[5], b[1])))
    hi = jnp.where(c4, jnp.where(c2, b[14], b[12]), jnp.where(c2, b[10], b[8]))
    lo = jnp.where(c4, jnp.where(c2, b[6], b[4]), jnp.where(c2, b[2], b[0]))
    c1 = ge(jnp.where(c8, hi, lo))
    pos = (jnp.where(c8, 8.0, 0.0) + jnp.where(c4, 4.0, 0.0)) + (jnp.where(c2, 2.0, 0.0) + jnp.where(c1, 1.0, 0.0))
    return jnp.where(ge(b[15]), 16.0, pos)


def _route_kernel(hb_ref, wpq_ref, keys_ref, rank_ref, e2_ref, n_ref, c1_ref):
    qh = _dot(hb_ref[...], wpq_ref[...]).astype(BF16)
    nk = PEER_NKEYS
    for h in range(PEER_HEADS):
        q1 = qh[:, (2 * h) * nk:(2 * h + 1) * nk]
        q2 = qh[:, (2 * h + 1) * nk:(2 * h + 2) * nk]
        s1 = _dot_nt(keys_ref[2 * h], q1)
        s2 = _dot_nt(keys_ref[2 * h + 1], q2)
        a = _top16_desc(s1)
        b = _top16_desc(s2)
        rank2 = _count(b, s2, strict=True)
        cands = []
        for k in range(PEER_TOPK):
            for l in range(PEER_TOPK // (k + 1)):
                cands.append(a[k] + b[l])
        pad = (-len(cands)) % 8
        cand = jnp.concatenate(cands + [jnp.full_like(a[0], NEG_INF)] * pad, axis=0)
        cur = cand
        for _ in range(PEER_TOPK - 1):
            m = jnp.max(cur, axis=0, keepdims=True)
            cur = jnp.where(cur == m, NEG_INF, cur)
        theta = jnp.max(cur, axis=0, keepdims=True)
        top = a[0] + b[0]
        z = jnp.sum(jnp.where(cand >= theta, jnp.exp(cand - top), 0.0), axis=0, keepdims=True)
        n = _count(b, theta - s1, strict=False)
        rows32 = slice(h * nk // 2, (h + 1) * nk // 2)
        rank_ref[0, rows32, :] = pltpu.bitcast(rank2.astype(BF16), jnp.uint32)
        e2_ref[0, rows32, :] = pltpu.bitcast(jnp.exp(s2 - b[0]).astype(BF16), jnp.uint32)
        tb = n.shape[1]
        n_ref[0, :, h, :, :] = _bf16_twice(n).reshape(nk // 8, 8, tb)
        c1_ref[0, :, h, :, :] = _bf16_twice(jnp.exp(s1 - a[0]) / z).reshape(nk // 8, 8, tb)


def _route(hb, wpq, keys, tb):
    t = hb.shape[0]
    nblk = t // tb
    rows = PEER_HEADS * PEER_NKEYS
    spec = lambda r: pl.BlockSpec((1, r, tb), lambda i: (i, 0, 0))
    shape = lambda r, dt: jax.ShapeDtypeStruct((nblk, r, tb), dt)
    key_dims = (PEER_NKEYS // 8, PEER_HEADS, 8, tb)
    key_spec = pl.BlockSpec((1,) + key_dims, lambda i: (i, 0, 0, 0, 0))
    key_shape = jax.ShapeDtypeStruct((nblk,) + key_dims, jnp.uint32)
    return pl.pallas_call(
        _route_kernel,
        grid=(nblk,),
        in_specs=[pl.BlockSpec((tb, D_MODEL), lambda i: (i, 0)), _full(wpq.shape), _full(keys.shape)],
        out_specs=[spec(rows // 2), spec(rows // 2), key_spec, key_spec],
        out_shape=[shape(rows // 2, jnp.uint32), shape(rows // 2, jnp.uint32), key_shape, key_shape],
        compiler_params=_cparams(("parallel",)),
        name="peer_route",
    )(hb, wpq, keys)


A_ROWS = 256


def _peer_kernel(hb_ref, rank_ref, e2_ref, n_ref, c1_ref, u_ref, vt_ref, r2_ref, l2g_ref, l2b_ref,
                 y_ref, acc_ref, a_ref, p_ref, *, ec, tb):
    c = pl.program_id(1)
    nk = PEER_NKEYS

    @pl.when(c == 0)
    def _():
        acc_ref[...] = jnp.zeros_like(acc_ref)

    hb = hb_ref[...]
    for r0 in range(0, ec, A_ROWS):
        a_ref[r0:r0 + A_ROWS, :] = _dot_nt(u_ref[r0:r0 + A_ROWS, :], hb)
    tbr = rank_ref.shape[2]
    for tt in range(tb // LANES):
        cols = slice(tt * LANES, (tt + 1) * LANES)
        rb = tt * LANES // tbr
        rcols = slice(tt * LANES % tbr, tt * LANES % tbr + LANES)
        for ii in range(ec // nk):
            w = None
            for h in range(PEER_HEADS):
                rows32 = slice(h * nk // 2, (h + 1) * nk // 2)
                rank = pltpu.bitcast(rank_ref[rb, rows32, rcols], BF16)
                e2 = pltpu.bitcast(e2_ref[rb, rows32, rcols], BF16)
                key = (rb, ii // 8, h, slice(ii % 8, ii % 8 + 1), rcols)
                n_row = pltpu.bitcast(jnp.broadcast_to(n_ref[key], (8, LANES)), BF16)
                c1_row = pltpu.bitcast(jnp.broadcast_to(c1_ref[key], (8, LANES)), BF16)
                n_all, c1_all = jnp.tile(n_row, (nk // 16, 1)), jnp.tile(c1_row, (nk // 16, 1))
                term = c1_all * jnp.where(rank < n_all, e2, jnp.zeros((), BF16))
                w = term if w is None else w + term
            av = a_ref[ii * nk:(ii + 1) * nk, cols].astype(BF16)
            gelu = (0.5 * av) * (1.0 + lax.erf(av * (2.0 ** -0.5)))
            p_ref[ii * nk:(ii + 1) * nk, cols] = gelu * w
    acc_ref[...] += _dot(vt_ref[...], p_ref[...])

    @pl.when(c == pl.num_programs(1) - 1)
    def _():
        ffn = acc_ref[...].T
        y_ref[...] = _layer_norm(r2_ref[...] + ffn, l2g_ref[...], l2b_ref[...])


def _peer(hb, rank2, e2, n, c1, u_b, vt_b, r2, l2g, l2b, tb, ec):
    t = hb.shape[0]
    nblk = t // tb
    rows = PEER_HEADS * PEER_NKEYS
    tbr = rank2.shape[2]
    nrb = tb // tbr
    route_spec = lambda r: pl.BlockSpec((nrb, r, tbr), lambda i, c: (i, 0, 0))
    key_spec = pl.BlockSpec((nrb, ec // (8 * PEER_NKEYS), PEER_HEADS, 8, tbr), lambda i, c: (i, c, 0, 0, 0))
    return pl.pallas_call(
        functools.partial(_peer_kernel, ec=ec, tb=tb),
        grid=(nblk, PEER_EXPERTS // ec),
        in_specs=[pl.BlockSpec((tb, D_MODEL), lambda i, c: (i, 0)),
                  route_spec(rows // 2), route_spec(rows // 2), key_spec, key_spec,
                  pl.BlockSpec((ec, D_MODEL), lambda i, c: (c, 0)),
                  pl.BlockSpec((D_MODEL, ec), lambda i, c: (0, c)),
                  pl.BlockSpec((tb, D_MODEL), lambda i, c: (i, 0)),
                  _full(l2g.shape), _full(l2b.shape)],
        out_specs=pl.BlockSpec((tb, D_MODEL), lambda i, c: (i, 0)),
        out_shape=jax.ShapeDtypeStruct((t, D_MODEL), F32),
        scratch_shapes=[pltpu.VMEM((D_MODEL, tb), F32), pltpu.VMEM((ec, tb), F32), pltpu.VMEM((ec, tb), BF16)],
        compiler_params=_cparams(("parallel", "arbitrary")),
        name="peer_experts",
    )(hb, rank2, e2, n, c1, u_b, vt_b, r2, l2g, l2b)


def _block_sizes(bsz, seq):
    t = bsz * seq
    return dict(
        tb_in=min(512, t),
        tbg=min(512, seq),
        tm=min(1024, seq), tk=min(1024, seq),
        tb_merge=min(256, t),
        tb_route=min(512, t),
        tb_peer=min(512, t),
        ec=16 * PEER_NKEYS,
    )


def _prep_weights(w_in, w_af2, b_af, w_ab2, b_ab, gla_norm_g, w_ao, w_fo, w_mg, b_mg, w_o,
                  ln1_g, ln1_b, w_pq, peer_keys, peer_u, peer_v, w_pg, w_pe, ln2_g, ln2_b):
    r = GLA_GATE_RANK
    zeros = jnp.zeros((r, QK_WIDTH), F32)
    w2 = jnp.concatenate([jnp.concatenate([w_af2, zeros], axis=1),
                          jnp.concatenate([zeros, w_ab2], axis=1)], axis=0)
    return dict(
        wq=w_in[:, :OFF_AF].astype(BF16),
        wg=w_in[:, OFF_AF:OFF_F].astype(BF16),
        wf=w_in[:, OFF_F:].astype(BF16),
        w2=w2.astype(BF16),
        b2=jnp.concatenate([b_af, b_ab])[None, :],
        gn=gla_norm_g[None, :],
        wao=w_ao.astype(BF16), wfo=w_fo.astype(BF16), wmg=w_mg.astype(BF16), bmg=b_mg[None, :],
        wo=w_o.astype(BF16), l1g=ln1_g[None, :], l1b=ln1_b[None, :],
        wpq=w_pq.astype(BF16),
        keys=peer_keys.reshape(PEER_HEADS * 2, PEER_NKEYS, PEER_QDIM // 2).astype(BF16),
        u=peer_u.astype(BF16), vt=peer_v.astype(BF16).T,
        wpg=w_pg.astype(BF16), wpe=w_pe.astype(BF16), l2g=ln2_g[None, :], l2b=ln2_b[None, :],
    )


def _encoder_layer(x, p, w):
    bsz, seq, d = x.shape
    t = bsz * seq
    bs = _block_sizes(bsz, seq)
    x2 = x.reshape(t, d)
    p2 = p.reshape(t, PLE_DIM)
    qkvr, la, f = _inproj(x2, w["wq"], w["wg"], w["wf"], w["w2"], w["b2"], bs["tb_in"])
    o_f, o_b = _gla(qkvr, la, bsz, seq, bs["tbg"])
    fr = _fnet(f, bsz, seq, bs["tm"], bs["tk"])
    hb, r2 = _merge(x2, o_f, o_b, qkvr, fr, p2, w["gn"], w["wao"], w["wfo"], w["wmg"], w["bmg"],
                    w["wo"], w["l1g"], w["l1b"], w["wpg"], w["wpe"], bs["tb_merge"])
    rank2, e2, n, c1 = _route(hb, w["wpq"], w["keys"], bs["tb_route"])
    y = _peer(hb, rank2, e2, n, c1, w["u"], w["vt"], r2, w["l2g"], w["l2b"], bs["tb_peer"], bs["ec"])
    return y.reshape(bsz, seq, d)


def kernel(x_prompt, x_sample, p_prompt, p_sample, w_in, w_af2, b_af, w_ab2, b_ab, gla_norm_g, w_ao, w_fo, w_mg, b_mg, w_o, ln1_g, ln1_b, w_pq, peer_keys, peer_u, peer_v, w_pg, w_pe, ln2_g, ln2_b):
    params = (w_in, w_af2, b_af, w_ab2, b_ab, gla_norm_g, w_ao, w_fo, w_mg, b_mg, w_o,
              ln1_g, ln1_b, w_pq, peer_keys, peer_u, peer_v, w_pg, w_pe, ln2_g, ln2_b)
    xs = (x_prompt, x_sample)
    ps = (p_prompt, p_sample)
    for i in range(DEPTH):
        w = _prep_weights(*[a[i] for a in params])
        xs = tuple(_encoder_layer(x, p[i], w) for x, p in zip(xs, ps))
    return xs
```

```python
import functools
import math

import jax
import jax.numpy as jnp
from jax import lax
from jax.experimental import pallas as pl
from jax.experimental.pallas import tpu as pltpu

F32 = jnp.float32
BF16 = jnp.bfloat16

D_MODEL = 1024
DEPTH = 1
GLA_HEADS = 4
GLA_DK = 64
GLA_DV = 128
GLA_GATE_RANK = 16
GLA_GATE_NORMALIZER = 16.0
GLA_CHUNK = 64
FNET_GROUPS = 4
FNET_GROUP_DIM = 128
PEER_HEADS = 8
PEER_NKEYS = 128
PEER_EXPERTS = PEER_NKEYS * PEER_NKEYS
PEER_QDIM = 256
PEER_TOPK = 16
PLE_DIM = 256
LN_EPS = 1e-5
RMS_EPS = 1e-6
DEEPNORM_ALPHA = (2.0 * DEPTH) ** 0.25

QK_WIDTH = GLA_HEADS * GLA_DK
V_WIDTH = GLA_HEADS * GLA_DV
F_WIDTH = FNET_GROUPS * FNET_GROUP_DIM
OFF_AF = 2 * QK_WIDTH + 2 * V_WIDTH
OFF_F = OFF_AF + 2 * GLA_GATE_RANK

VMEM_LIMIT_BYTES = 56 * 1024 * 1024
LANES = 128

NEG_INF = float("-inf")


def _cparams(sem, flags=None):
    return pltpu.CompilerParams(dimension_semantics=sem, vmem_limit_bytes=VMEM_LIMIT_BYTES, flags=flags)


def _full(shape):
    nd = len(shape)
    return pl.BlockSpec(shape, lambda *_: (0,) * nd)


def _dot(a, b):
    return jnp.dot(a, b, preferred_element_type=F32)


def _dot_nt(a, b):
    return lax.dot_general(a, b, (((1,), (1,)), ((), ())), preferred_element_type=F32)


def _dot_tn(a, b):
    return lax.dot_general(a, b, (((0,), (0,)), ((), ())), preferred_element_type=F32)


def _layer_norm(x, g, b):
    mu = jnp.mean(x, axis=-1, keepdims=True)
    xc = x - mu
    var = jnp.mean(xc * xc, axis=-1, keepdims=True)
    return xc * lax.rsqrt(var + LN_EPS) * g + b


def _inproj_kernel(x_ref, wq_ref, wg_ref, wf_ref, w2_ref, b2_ref, qkvr_ref, la_ref, f_ref):
    xb = x_ref[...].astype(BF16)
    qkvr_ref[...] = _dot(xb, wq_ref[...])
    g = _dot(xb, wg_ref[...])
    z = _dot(g.astype(BF16), w2_ref[...]) + b2_ref[...]
    log_sig = jnp.minimum(z, 0.0) - jnp.log1p(jnp.exp(-jnp.abs(z)))
    la_ref[...] = log_sig * (1.0 / GLA_GATE_NORMALIZER)
    f_ref[...] = _dot(xb, wf_ref[...]).astype(BF16)


def _inproj(x2, wq, wg, wf, w2, b2, tb):
    t = x2.shape[0]
    nq = wq.shape[1]
    return pl.pallas_call(
        _inproj_kernel,
        grid=(t // tb,),
        in_specs=[
            pl.BlockSpec((tb, D_MODEL), lambda i: (i, 0)),
            _full(wq.shape), _full(wg.shape), _full(wf.shape), _full(w2.shape), _full(b2.shape),
        ],
        out_specs=[
            pl.BlockSpec((tb, nq), lambda i: (i, 0)),
            pl.BlockSpec((tb, 2 * QK_WIDTH), lambda i: (i, 0)),
            pl.BlockSpec((tb, F_WIDTH), lambda i: (i, 0)),
        ],
        out_shape=[
            jax.ShapeDtypeStruct((t, nq), F32),
            jax.ShapeDtypeStruct((t, 2 * QK_WIDTH), F32),
            jax.ShapeDtypeStruct((t, F_WIDTH), BF16),
        ],
        compiler_params=_cparams(("parallel",)),
        name="inproj",
    )(x2, wq, wg, wf, w2, b2)


def _gla_chunk(q, k, v, la, tri, smask, hmask, bmask, st_ref, mid, last):
    c = GLA_CHUNK
    la_hi = la.astype(BF16)
    la_lo = (la - la_hi.astype(F32)).astype(BF16)
    bc = _dot(tri, la_hi) + _dot(tri, la_lo)
    b_mid = bc[mid:mid + 1, :]
    b_last = bc[last:last + 1, :]
    qs = q * (GLA_DK ** -0.5)
    vb = v.astype(BF16)
    qd = qs * jnp.exp(bc - b_mid)
    kd = (k * jnp.exp(b_mid - bc)).astype(BF16)
    qstack = jnp.concatenate(
        [jnp.where(hmask[h:h + 1, :] > 0.0, qd, 0.0) for h in range(GLA_HEADS)], axis=0).astype(BF16)
    scores = _dot_nt(qstack, kd)
    scores = jnp.where(smask > 0.0, scores, 0.0).astype(BF16)
    oi_all = _dot(scores, vb)
    o_intra = jnp.concatenate(
        [oi_all[h * c:(h + 1) * c, h * GLA_DV:(h + 1) * GLA_DV] for h in range(GLA_HEADS)], axis=1)
    q_in = (qs * jnp.exp(bc)).astype(BF16)
    st = st_ref[...]
    o_inter = _dot_nt(q_in, st.astype(BF16))
    k_st = (k * jnp.exp(b_last - bc)).astype(BF16)
    upd = _dot_tn(vb, k_st)
    st_ref[...] = st * jnp.exp(b_last) + jnp.where(bmask > 0.0, upd, 0.0)
    return o_intra + o_inter


def _gla_kernel(qf_ref, kf_ref, vf_ref, laf_ref, qb_ref, kb_ref, vb_ref, lab_ref,
                trif_ref, trib_ref, smf_ref, smb_ref, hm_ref, bm_ref,
                of_ref, ob_ref, stf_ref, stb_ref, *, nchunk):
    @pl.when(pl.program_id(1) == 0)
    def _():
        stf_ref[...] = jnp.zeros_like(stf_ref)
        stb_ref[...] = jnp.zeros_like(stb_ref)

    c = GLA_CHUNK
    trif = trif_ref[...]
    trib = trib_ref[...]
    smf = smf_ref[...]
    smb = smb_ref[...]
    hm = hm_ref[...]
    bm = bm_ref[...]

    def body(j, carry):
        rf = pl.ds(pl.multiple_of(j * c, c), c)
        of_ref[rf, :] = _gla_chunk(qf_ref[rf, :], kf_ref[rf, :], vf_ref[rf, :], laf_ref[rf, :],
                                   trif, smf, hm, bm, stf_ref, c // 2 - 1, c - 1)
        rb = pl.ds(pl.multiple_of((nchunk - 1 - j) * c, c), c)
        ob_ref[rb, :] = _gla_chunk(qb_ref[rb, :], kb_ref[rb, :], vb_ref[rb, :], lab_ref[rb, :],
                                   trib, smb, hm, bm, stb_ref, c // 2, 0)
        return carry

    lax.fori_loop(0, nchunk, body, 0, unroll=8)


def _gla_consts():
    c = GLA_CHUNK
    r = jnp.arange(c)
    trif = (r[:, None] >= r[None, :]).astype(BF16)
    trib = (r[None, :] >= r[:, None]).astype(BF16)
    rr = jnp.arange(GLA_HEADS * c) % c
    smf = (rr[:, None] >= r[None, :]).astype(F32)
    smb = (r[None, :] > rr[:, None]).astype(F32)
    hm = (jnp.arange(QK_WIDTH)[None, :] // GLA_DK == jnp.arange(GLA_HEADS)[:, None]).astype(F32)
    hm = jnp.concatenate([hm, jnp.zeros((8 - GLA_HEADS, QK_WIDTH), F32)], axis=0)
    bm = (jnp.arange(V_WIDTH)[:, None] // GLA_DV == jnp.arange(QK_WIDTH)[None, :] // GLA_DK).astype(F32)
    return trif, trib, smf, smb, hm, bm


def _gla(qkvr, la, bsz, seq, tbg):
    t = bsz * seq
    nblk = seq // tbg
    nchunk = tbg // GLA_CHUNK
    consts = _gla_consts()
    fwd = lambda col: (lambda b, i: (b * nblk + i, col))
    bwd = lambda col: (lambda b, i: (b * nblk + nblk - 1 - i, col))
    in_specs = [
        pl.BlockSpec((tbg, QK_WIDTH), fwd(0)),
        pl.BlockSpec((tbg, QK_WIDTH), fwd(1)),
        pl.BlockSpec((tbg, V_WIDTH), fwd(1)),
        pl.BlockSpec((tbg, QK_WIDTH), fwd(0)),
        pl.BlockSpec((tbg, QK_WIDTH), bwd(0)),
        pl.BlockSpec((tbg, QK_WIDTH), bwd(1)),
        pl.BlockSpec((tbg, V_WIDTH), bwd(1)),
        pl.BlockSpec((tbg, QK_WIDTH), bwd(1)),
    ] + [_full(a.shape) for a in consts]
    return pl.pallas_call(
        functools.partial(_gla_kernel, nchunk=nchunk),
        grid=(bsz, nblk),
        in_specs=in_specs,
        out_specs=[pl.BlockSpec((tbg, V_WIDTH), fwd(0)), pl.BlockSpec((tbg, V_WIDTH), bwd(0))],
        out_shape=[jax.ShapeDtypeStruct((t, V_WIDTH), F32), jax.ShapeDtypeStruct((t, V_WIDTH), F32)],
        scratch_shapes=[pltpu.VMEM((V_WIDTH, QK_WIDTH), F32), pltpu.VMEM((V_WIDTH, QK_WIDTH), F32)],
        compiler_params=_cparams(("parallel", "arbitrary")),
        name="gla",
    )(qkvr, qkvr, qkvr, la, qkvr, qkvr, qkvr, la, *consts)


def _dft_kernel(c_ref, s_ref, f_ref, c128_ref, s128_ref, o_ref, p_acc, q_acc):
    k = pl.program_id(2)

    @pl.when(k == 0)
    def _():
        p_acc[...] = jnp.zeros_like(p_acc)
        q_acc[...] = jnp.zeros_like(q_acc)

    fb = f_ref[...]
    p_acc[...] += _dot(c_ref[...], fb)
    q_acc[...] += _dot(s_ref[...], fb)

    @pl.when(k == pl.num_programs(2) - 1)
    def _():
        c128 = c128_ref[...]
        s128 = s128_ref[...]
        for g in range(FNET_GROUPS):
            sl = slice(g * FNET_GROUP_DIM, (g + 1) * FNET_GROUP_DIM)
            pg = p_acc[:, sl].astype(BF16)
            qg = q_acc[:, sl].astype(BF16)
            o_ref[:, sl] = (_dot(pg, c128) - _dot(qg, s128)).astype(BF16)


def _dft_tables(n, split):
    k = jnp.arange(n, dtype=jnp.int32)
    m_hi = jnp.arange(n // split, dtype=jnp.int32) * split
    m_lo = jnp.arange(split, dtype=jnp.int32)
    w = 2.0 * math.pi / n
    ang_hi = ((m_hi[:, None] * k[None, :]) % n).astype(F32) * w
    ang_lo = ((m_lo[:, None] * k[None, :]) % n).astype(F32) * w
    ch, sh = jnp.cos(ang_hi)[:, None, :], jnp.sin(ang_hi)[:, None, :]
    cl, sl = jnp.cos(ang_lo)[None, :, :], jnp.sin(ang_lo)[None, :, :]
    c = (ch * cl - sh * sl).reshape(n, n)
    s = (sh * cl + ch * sl).reshape(n, n)
    return c.astype(BF16), s.astype(BF16)


def _small_dft_tables(n, scale):
    k = jnp.arange(n, dtype=jnp.int32)
    ang = ((k[:, None] * k[None, :]) % n).astype(F32) * (2.0 * math.pi / n)
    return (jnp.cos(ang) * scale).astype(BF16), (jnp.sin(ang) * scale).astype(BF16)


def _fnet(f, bsz, seq, tm, tk):
    t = bsz * seq
    c_l, s_l = _dft_tables(seq, 64)
    c128, s128 = _small_dft_tables(FNET_GROUP_DIM, (seq * FNET_GROUP_DIM) ** -0.5)
    nm, nk = seq // tm, seq // tk
    return pl.pallas_call(
        _dft_kernel,
        grid=(bsz, nm, nk),
        in_specs=[
            pl.BlockSpec((tm, tk), lambda b, m, k: (m, k)),
            pl.BlockSpec((tm, tk), lambda b, m, k: (m, k)),
            pl.BlockSpec((tk, F_WIDTH), lambda b, m, k: (b * nk + k, 0)),
            _full(c128.shape), _full(s128.shape),
        ],
        out_specs=pl.BlockSpec((tm, F_WIDTH), lambda b, m, k: (b * nm + m, 0)),
        out_shape=jax.ShapeDtypeStruct((t, F_WIDTH), BF16),
        scratch_shapes=[pltpu.VMEM((tm, F_WIDTH), F32), pltpu.VMEM((tm, F_WIDTH), F32)],
        compiler_params=_cparams(("parallel", "parallel", "arbitrary")),
        name="fnet_dft",
    )(c_l, s_l, f, c128, s128)


def _merge_kernel(x_ref, of_ref, ob_ref, r_ref, fr_ref, p_ref,
                  gn_ref, wao_ref, wfo_ref, wmg_ref, bmg_ref, wo_ref, l1g_ref, l1b_ref,
                  wpg_ref, wpe_ref, hb_ref, r2_ref):
    x = x_ref[...]
    xb = x.astype(BF16)
    o = of_ref[...] + ob_ref[...]
    parts = []
    for h in range(GLA_HEADS):
        oh = o[:, h * GLA_DV:(h + 1) * GLA_DV]
        ms = jnp.mean(oh * oh, axis=-1, keepdims=True)
        parts.append(oh * lax.rsqrt(ms + RMS_EPS) * gn_ref[...])
    on = jnp.concatenate(parts, axis=1)
    r = r_ref[...]
    og = (on * (r * jax.nn.sigmoid(r))).astype(BF16)
    branch_gla = _dot(og, wao_ref[...])
    branch_fnet = _dot(fr_ref[...], wfo_ref[...])
    gates = jax.nn.sigmoid(_dot(xb, wmg_ref[...]) + bmg_ref[...])
    merged = gates[:, :D_MODEL] * branch_gla + gates[:, D_MODEL:] * branch_fnet
    mix = _dot(merged.astype(BF16), wo_ref[...])
    h1 = _layer_norm(DEEPNORM_ALPHA * x + mix, l1g_ref[...], l1b_ref[...])
    h1b = h1.astype(BF16)
    hb_ref[...] = h1b
    ple = jax.nn.sigmoid(_dot(h1b, wpg_ref[...])) * _dot(p_ref[...].astype(BF16), wpe_ref[...])
    r2_ref[...] = DEEPNORM_ALPHA * h1 + ple


def _merge(x2, o_f, o_b, qkvr, fr, p2, gn, wao, wfo, wmg, bmg, wo, l1g, l1b, wpg, wpe, tb):
    t = x2.shape[0]
    row = lambda w: pl.BlockSpec((tb, w), lambda i: (i, 0))
    weights = (gn, wao, wfo, wmg, bmg, wo, l1g, l1b, wpg, wpe)
    return pl.pallas_call(
        _merge_kernel,
        grid=(t // tb,),
        in_specs=[row(D_MODEL), row(V_WIDTH), row(V_WIDTH),
                  pl.BlockSpec((tb, V_WIDTH), lambda i: (i, 2)),
                  row(F_WIDTH), row(PLE_DIM)] + [_full(w.shape) for w in weights],
        out_specs=[row(D_MODEL), row(D_MODEL)],
        out_shape=[jax.ShapeDtypeStruct((t, D_MODEL), BF16), jax.ShapeDtypeStruct((t, D_MODEL), F32)],
        compiler_params=_cparams(("parallel",)),
        name="merge_ln1",
    )(x2, o_f, o_b, qkvr, fr, p2, *weights)


def _sort16_pairs():
    def merge(lo, hi, r):
        step = r * 2
        if step < hi - lo:
            yield from merge(lo, hi, step)
            yield from merge(lo + r, hi, step)
            yield from ((i, i + r) for i in range(lo + r, hi - r, step))
        else:
            yield (lo, lo + r)

    def sort(lo, hi):
        if hi - lo >= 1:
            mid = lo + (hi - lo) // 2
            yield from sort(lo, mid)
            yield from sort(mid + 1, hi)
            yield from merge(lo, hi, 1)

    return tuple(sort(0, PEER_TOPK - 1))


def _top16_desc(s):
    assert s.shape[0] == 8 * PEER_TOPK
    v = [s[8 * k:8 * (k + 1), :] for k in range(PEER_TOPK)]
    for i, j in _sort16_pairs():
        v[i], v[j] = jnp.maximum(v[i], v[j]), jnp.minimum(v[i], v[j])
    vals = []
    for r in range(PEER_TOPK):
        m = jnp.max(v[0], axis=0, keepdims=True)
        vals.append(m)
        if r == PEER_TOPK - 1:
            break
        hit = v[0] == m
        last = PEER_TOPK - 1 - r
        for k in range(last):
            v[k] = jnp.where(hit, v[k + 1], v[k])
        v[last] = jnp.where(hit, NEG_INF, v[last])
    return vals


def _count(b, x, strict):
    assert len(b) == 16
    ge = (lambda u: u > x) if strict else (lambda u: u >= x)
    c8 = ge(b[7])
    c4 = ge(jnp.where(c8, b[11], b[3]))
    c2 = ge(jnp.where(c8, jnp.where(c4, b[13], b[9]), jnp.where(c4, b[5], b[1])))
    hi = jnp.where(c4, jnp.where(c2, b[14], b[12]), jnp.where(c2, b[10], b[8]))
    lo = jnp.where(c4, jnp.where(c2, b[6], b[4]), jnp.where(c2, b[2], b[0]))
    c1 = ge(jnp.where(c8, hi, lo))
    pos = (jnp.where(c8, 8.0, 0.0) + jnp.where(c4, 4.0, 0.0)) + (jnp.where(c2, 2.0, 0.0) + jnp.where(c1, 1.0, 0.0))
    return jnp.where(ge(b[15]), 16.0, pos)


def _route_kernel(hb_ref, wpq_ref, keys_ref, rank_ref, e2_ref, n_ref, c1_ref):
    qh = _dot(hb_ref[...], wpq_ref[...]).astype(BF16)
    nk = PEER_NKEYS
    for h in range(PEER_HEADS):
        q1 = qh[:, (2 * h) * nk:(2 * h + 1) * nk]
        q2 = qh[:, (2 * h + 1) * nk:(2 * h + 2) * nk]
        s1 = _dot_nt(keys_ref[2 * h], q1)
        s2 = _dot_nt(keys_ref[2 * h + 1], q2)
        a = _top16_desc(s1)
        b = _top16_desc(s2)
        rank2 = _count(b, s2, strict=True)
        cands = []
        for k in range(PEER_TOPK):
            for l in range(PEER_TOPK // (k + 1)):
                cands.append(a[k] + b[l])
        pad = (-len(cands)) % 8
        cand = jnp.concatenate(cands + [jnp.full_like(a[0], NEG_INF)] * pad, axis=0)
        cur = cand
        for _ in range(PEER_TOPK - 1):
            m = jnp.max(cur, axis=0, keepdims=True)
            cur = jnp.where(cur == m, NEG_INF, cur)
        theta = jnp.max(cur, axis=0, keepdims=True)
        top = a[0] + b[0]
        z = jnp.sum(jnp.where(cand >= theta, jnp.exp(cand - top), 0.0), axis=0, keepdims=True)
        n = _count(b, theta - s1, strict=False)
        rows32 = slice(h * nk // 2, (h + 1) * nk // 2)
        rank_ref[0, rows32, :] = pltpu.bitcast(rank2.astype(BF16), jnp.uint32)
        e2_ref[0, rows32, :] = pltpu.bitcast(jnp.exp(s2 - b[0]).astype(BF16), jnp.uint32)
        tb = n.shape[1]
        n_ref[0, :, h, :, :] = n.reshape(nk // 8, 8, tb)
        c1_ref[0, :, h, :, :] = (jnp.exp(s1 - a[0]) / z).reshape(nk // 8, 8, tb)


def _route(hb, wpq, keys, tb):
    t = hb.shape[0]
    nblk = t // tb
    rows = PEER_HEADS * PEER_NKEYS
    spec = lambda r: pl.BlockSpec((1, r, tb), lambda i: (i, 0, 0))
    shape = lambda r, dt: jax.ShapeDtypeStruct((nblk, r, tb), dt)
    key_dims = (PEER_NKEYS // 8, PEER_HEADS, 8, tb)
    key_spec = pl.BlockSpec((1,) + key_dims, lambda i: (i, 0, 0, 0, 0))
    key_shape = jax.ShapeDtypeStruct((nblk,) + key_dims, F32)
    return pl.pallas_call(
        _route_kernel,
        grid=(nblk,),
        in_specs=[pl.BlockSpec((tb, D_MODEL), lambda i: (i, 0)), _full(wpq.shape), _full(keys.shape)],
        out_specs=[spec(rows // 2), spec(rows // 2), key_spec, key_spec],
        out_shape=[shape(rows // 2, jnp.uint32), shape(rows // 2, jnp.uint32), key_shape, key_shape],
        compiler_params=_cparams(("parallel",)),
        name="peer_route",
    )(hb, wpq, keys)


A_ROWS = 256


def _peer_kernel(hb_ref, rank_ref, e2_ref, n_ref, c1_ref, u_ref, vt_ref, r2_ref, l2g_ref, l2b_ref,
                 y_ref, acc_ref, a_ref, p_ref, *, ec, tb):
    c = pl.program_id(1)
    nk = PEER_NKEYS

    @pl.when(c == 0)
    def _():
        acc_ref[...] = jnp.zeros_like(acc_ref)

    hb = hb_ref[...]
    for r0 in range(0, ec, A_ROWS):
        a_ref[r0:r0 + A_ROWS, :] = _dot_nt(u_ref[r0:r0 + A_ROWS, :], hb)
    tbr = rank_ref.shape[2]
    for tt in range(tb // LANES):
        cols = slice(tt * LANES, (tt + 1) * LANES)
        rb = tt * LANES // tbr
        rcols = slice(tt * LANES % tbr, tt * LANES % tbr + LANES)
        n8 = [[n_ref[rb, g, h, :, rcols] for h in range(PEER_HEADS)] for g in range(ec // (8 * nk))]
        c18 = [[c1_ref[rb, g, h, :, rcols] for h in range(PEER_HEADS)] for g in range(ec // (8 * nk))]
        for ii in range(ec // nk):
            w = None
            for h in range(PEER_HEADS):
                rows32 = slice(h * nk // 2, (h + 1) * nk // 2)
                rank = pltpu.bitcast(rank_ref[rb, rows32, rcols], BF16)
                e2 = pltpu.bitcast(e2_ref[rb, rows32, rcols], BF16)
                n_row = n8[ii // 8][h][ii % 8:ii % 8 + 1, :].astype(BF16)
                c1_row = c18[ii // 8][h][ii % 8:ii % 8 + 1, :].astype(BF16)
                term = c1_row * jnp.where(rank < n_row, e2, jnp.zeros((), BF16))
                w = term if w is None else w + term
            av = a_ref[ii * nk:(ii + 1) * nk, cols].astype(BF16)
            gelu = (0.5 * av) * (1.0 + lax.erf(av * (2.0 ** -0.5)))
            p_ref[ii * nk:(ii + 1) * nk, cols] = gelu * w
    acc_ref[...] += _dot(vt_ref[...], p_ref[...])

    @pl.when(c == pl.num_programs(1) - 1)
    def _():
        ffn = acc_ref[...].T
        y_ref[...] = _layer_norm(r2_ref[...] + ffn, l2g_ref[...], l2b_ref[...])


def _peer(hb, rank2, e2, n, c1, u_b, vt_b, r2, l2g, l2b, tb, ec):
    t = hb.shape[0]
    nblk = t // tb
    rows = PEER_HEADS * PEER_NKEYS
    tbr = rank2.shape[2]
    nrb = tb // tbr
    route_spec = lambda r: pl.BlockSpec((nrb, r, tbr), lambda i, c: (i, 0, 0))
    key_spec = pl.BlockSpec((nrb, ec // (8 * PEER_NKEYS), PEER_HEADS, 8, tbr), lambda i, c: (i, c, 0, 0, 0))
    return pl.pallas_call(
        functools.partial(_peer_kernel, ec=ec, tb=tb),
        grid=(nblk, PEER_EXPERTS // ec),
        in_specs=[pl.BlockSpec((tb, D_MODEL), lambda i, c: (i, 0)),
                  route_spec(rows // 2), route_spec(rows // 2), key_spec, key_spec,
                  pl.BlockSpec((ec, D_MODEL), lambda i, c: (c, 0)),
                  pl.BlockSpec((D_MODEL, ec), lambda i, c: (0, c)),
                  pl.BlockSpec((tb, D_MODEL), lambda i, c: (i, 0)),
                  _full(l2g.shape), _full(l2b.shape)],
        out_specs=pl.BlockSpec((tb, D_MODEL), lambda i, c: (i, 0)),
        out_shape=jax.ShapeDtypeStruct((t, D_MODEL), F32),
        scratch_shapes=[pltpu.VMEM((D_MODEL, tb), F32), pltpu.VMEM((ec, tb), F32), pltpu.VMEM((ec, tb), BF16)],
        compiler_params=_cparams(("parallel", "arbitrary")),
        name="peer_experts",
    )(hb, rank2, e2, n, c1, u_b, vt_b, r2, l2g, l2b)


def _block_sizes(bsz, seq):
    t = bsz * seq
    return dict(
        tb_in=min(512, t),
        tbg=min(512, seq),
        tm=min(2048, seq), tk=min(1024, seq),
        tb_merge=min(256, t),
        tb_route=min(512, t),
        tb_peer=min(512, t),
        ec=16 * PEER_NKEYS,
    )


def _prep_weights(w_in, w_af2, b_af, w_ab2, b_ab, gla_norm_g, w_ao, w_fo, w_mg, b_mg, w_o,
                  ln1_g, ln1_b, w_pq, peer_keys, peer_u, peer_v, w_pg, w_pe, ln2_g, ln2_b):
    r = GLA_GATE_RANK
    zeros = jnp.zeros((r, QK_WIDTH), F32)
    w2 = jnp.concatenate([jnp.concatenate([w_af2, zeros], axis=1),
                          jnp.concatenate([zeros, w_ab2], axis=1)], axis=0)
    return dict(
        wq=w_in[:, :OFF_AF].astype(BF16),
        wg=w_in[:, OFF_AF:OFF_F].astype(BF16),
        wf=w_in[:, OFF_F:].astype(BF16),
        w2=w2.astype(BF16),
        b2=jnp.concatenate([b_af, b_ab])[None, :],
        gn=gla_norm_g[None, :],
        wao=w_ao.astype(BF16), wfo=w_fo.astype(BF16), wmg=w_mg.astype(BF16), bmg=b_mg[None, :],
        wo=w_o.astype(BF16), l1g=ln1_g[None, :], l1b=ln1_b[None, :],
        wpq=w_pq.astype(BF16),
        keys=peer_keys.reshape(PEER_HEADS * 2, PEER_NKEYS, PEER_QDIM // 2).astype(BF16),
        u=peer_u.astype(BF16), vt=peer_v.astype(BF16).T,
        wpg=w_pg.astype(BF16), wpe=w_pe.astype(BF16), l2g=ln2_g[None, :], l2b=ln2_b[None, :],
    )


def _encoder_layer(x, p, w):
    bsz, seq, d = x.shape
    t = bsz * seq
    bs = _block_sizes(bsz, seq)
    x2 = x.reshape(t, d)
    p2 = p.reshape(t, PLE_DIM)
    qkvr, la, f = _inproj(x2, w["wq"], w["wg"], w["wf"], w["w2"], w["b2"], bs["tb_in"])
    o_f, o_b = _gla(qkvr, la, bsz, seq, bs["tbg"])
    fr = _fnet(f, bsz, seq, bs["tm"], bs["tk"])
    hb, r2 = _merge(x2, o_f, o_b, qkvr, fr, p2, w["gn"], w["wao"], w["wfo"], w["wmg"], w["bmg"],
                    w["wo"], w["l1g"], w["l1b"], w["wpg"], w["wpe"], bs["tb_merge"])
    rank2, e2, n, c1 = _route(hb, w["wpq"], w["keys"], bs["tb_route"])
    y = _peer(hb, rank2, e2, n, c1, w["u"], w["vt"], r2, w["l2g"], w["l2b"], bs["tb_peer"], bs["ec"])
    return y.reshape(bsz, seq, d)


def kernel(x_prompt, x_sample, p_prompt, p_sample, w_in, w_af2, b_af, w_ab2, b_ab, gla_norm_g, w_ao, w_fo, w_mg, b_mg, w_o, ln1_g, ln1_b, w_pq, peer_keys, peer_u, peer_v, w_pg, w_pe, ln2_g, ln2_b):
    params = (w_in, w_af2, b_af, w_ab2, b_ab, gla_norm_g, w_ao, w_fo, w_mg, b_mg, w_o,
              ln1_g, ln1_b, w_pq, peer_keys, peer_u, peer_v, w_pg, w_pe, ln2_g, ln2_b)
    xs = (x_prompt, x_sample)
    ps = (p_prompt, p_sample)
    for i in range(DEPTH):
        w = _prep_weights(*[a[i] for a in params])
        xs = tuple(_encoder_layer(x, p[i], w) for x, p in zip(xs, ps))
    return xs
```
